```python
import functools
import jax, jax.numpy as jnp
from jax import lax
import numpy as np

D_MODEL = 1024
BATCH = 1
SEQ = 16384
DEPTH = 2
DEC_BATCH = 32
DEC_SEQ = 1
PAST_LEN = 16384
PAGE_SIZE = 128

SB_HEAD_DIM = 64
SB_WIDTH = D_MODEL // 2
SB_HEADS = SB_WIDTH // SB_HEAD_DIM
Q_BLOCK = 128
SB_BIAS_INIT = -8.0
RW_HEAD_DIM = 64
RW_WIDTH = D_MODEL // 2
RW_HEADS = RW_WIDTH // RW_HEAD_DIM
RW_DECAY_RANK = 64
RW_A_RANK = 64
RW_SHIFT_WIDTH = 3 * RW_WIDTH + RW_DECAY_RANK + RW_A_RANK
RW_SPLITS = (RW_WIDTH, 2 * RW_WIDTH, 3 * RW_WIDTH, 3 * RW_WIDTH + RW_DECAY_RANK)
GN_EPS = 64e-5
MEM_TOKENS = 256
MEM_HEADS = 4
MEM_HEAD_DIM = 128
MEM_WIDTH = MEM_HEADS * MEM_HEAD_DIM
N_BRANCH = 3
RMS_EPS = 1e-6
IN_SPLITS = (
    SB_WIDTH, 2 * SB_WIDTH, 3 * SB_WIDTH, 4 * SB_WIDTH,
    4 * SB_WIDTH + RW_SHIFT_WIDTH,
    4 * SB_WIDTH + RW_SHIFT_WIDTH + RW_WIDTH,
    4 * SB_WIDTH + RW_SHIFT_WIDTH + RW_WIDTH + MEM_WIDTH,
    4 * SB_WIDTH + RW_SHIFT_WIDTH + RW_WIDTH + 2 * MEM_WIDTH,
)
C_IN = IN_SPLITS[-1] + N_BRANCH * D_MODEL

kernel_name = "stickbreak_rwkv7_mem_gated_hybrid_step"


def _rmsnorm(x, g):
    xf = x.astype(jnp.float32)
    y = xf * lax.rsqrt(jnp.mean(xf * xf, axis=-1, keepdims=True) + RMS_EPS)
    return (y * g.astype(jnp.float32)).astype(x.dtype)


def _heads(t, n_heads):
    return t.reshape(t.shape[:-1] + (n_heads, t.shape[-1] // n_heads))


def _stick_breaking(q, k, v, bias, q_pos, k_pos):
    z = (jnp.einsum('bqhd,bkhd->bhqk', q, k).astype(jnp.float32) * (SB_HEAD_DIM ** -0.5)
         + bias.astype(jnp.float32)[None, :, None, None])
    visible = k_pos[None, :] < q_pos[:, None]
    log_keep = jnp.where(visible, jax.nn.log_sigmoid(-z), 0.0)
    between = lax.cumsum(log_keep, axis=3, reverse=True) - log_keep
    wts = jnp.where(visible, jnp.exp(jax.nn.log_sigmoid(z) + between), 0.0)
    return jnp.einsum('bhqk,bkhd->bqhd', wts.astype(v.dtype), v)


def _sb_prompt(q, k, v, bias):
    b, t = q.shape[:2]
    nb = t // Q_BLOCK
    qb = q.reshape(b, nb, Q_BLOCK, SB_HEADS, SB_HEAD_DIM).swapaxes(0, 1)
    k_pos = jnp.arange(t)

    def one_block(args):
        q_blk, i = args
        return _stick_breaking(q_blk, k, v, bias, i * Q_BLOCK + jnp.arange(Q_BLOCK), k_pos)

    o = lax.map(one_block, (qb, jnp.arange(nb)))
    return o.swapaxes(0, 1).reshape(b, t, SB_HEADS, SB_HEAD_DIM)


def _sb_sample(q, k_new, v_new, bias, k_pool, v_pool, page_table):
    b, t = q.shape[:2]
    past = page_table.shape[1] * k_pool.shape[1]
    k_past = k_pool[page_table].reshape(b, past, SB_HEADS, SB_HEAD_DIM).astype(k_new.dtype)
    v_past = v_pool[page_table].reshape(b, past, SB_HEADS, SB_HEAD_DIM).astype(v_new.dtype)
    k_all = jnp.concatenate([k_past, k_new], axis=1)
    v_all = jnp.concatenate([v_past, v_new], axis=1)
    return _stick_breaking(q, k_all, v_all, bias, past + jnp.arange(t), jnp.arange(past + t))


def _mem_kv(mem, g, w_kv):
    kv = _rmsnorm(mem, g) @ w_kv
    mk, mv = jnp.split(kv, 2, axis=-1)
    return _heads(mk, MEM_HEADS), _heads(mv, MEM_HEADS)


def _mem_attend(q, mk, mv):
    s = jnp.einsum('bqhd,bmhd->bhqm', q, mk).astype(jnp.float32) * (MEM_HEAD_DIM ** -0.5)
    p = jax.nn.softmax(s, axis=-1)
    return jnp.einsum('bhqm,bmhd->bqhd', p.astype(mv.dtype), mv)


def _wkv_scan(r, decay, k, v, kk, a, s0):
    f32 = jnp.float32
    xs = tuple(jnp.moveaxis(t.astype(f32), 1, 0) for t in (r, decay, k, v, kk, a))

    def step(s, inp):
        r_t, w_t, k_t, v_t, kk_t, a_t = inp
        sa = jnp.einsum('bhvk,bhk->bhv', s, kk_t)
        s = (s * w_t[:, :, None, :] - sa[..., None] * (kk_t * a_t)[:, :, None, :]
             + v_t[..., None] * k_t[:, :, None, :])
        return s, jnp.einsum('bhvk,bhk->bhv', s, r_t)

    s, ys = lax.scan(step, s0.astype(f32), xs)
    return jnp.moveaxis(ys, 0, 1), s


def _rwkv_branch(p, prev_row, s0, mu, w0, w2, a0, a2, k_k, k_a, r_k, ln_g, ln_b):
    b, t = p.shape[:2]
    prev = jnp.concatenate([prev_row[:, None].astype(p.dtype), p[:, :-1]], axis=1)
    u = p + mu * (prev - p)
    r, k, v, wl, al = jnp.split(u, RW_SPLITS, axis=-1)
    w_log = -jax.nn.softplus(-(w0 + jnp.tanh(wl) @ w2)) - 0.5
    decay = jnp.exp(-jnp.exp(w_log.astype(jnp.float32)))
    a = jax.nn.sigmoid(a0 + al @ a2)
    kk = _heads(k * k_k, RW_HEADS).astype(jnp.float32)
    kk = kk / jnp.maximum(jnp.sqrt(jnp.sum(kk * kk, axis=-1, keepdims=True)), 1e-12)
    k = k * (1.0 + (a - 1.0) * k_a)
    rh, kh, vh = _heads(r, RW_HEADS), _heads(k, RW_HEADS), _heads(v, RW_HEADS)
    y, s_new = _wkv_scan(rh, _heads(decay, RW_HEADS), kh, vh, kk, _heads(a, RW_HEADS), s0)
    mean = jnp.mean(y, axis=-1, keepdims=True)
    var = jnp.mean(jnp.square(y - mean), axis=-1, keepdims=True)
    y = ((y - mean) * lax.rsqrt(var + GN_EPS)).reshape(b, t, RW_WIDTH)
    y = y * ln_g.astype(jnp.float32) + ln_b.astype(jnp.float32)
    bonus = jnp.sum((rh * kh * r_k).astype(jnp.float32), axis=-1, keepdims=True) * vh.astype(jnp.float32)
    y = y + bonus.reshape(b, t, RW_WIDTH)
    return y.astype(p.dtype), s_new, p[:, -1]


def _layer(x, mk, mv, s0, prev_row, sb_fn, lp):
    (norm_g, w_in, sb_bias, shift_mu, rw_w0, rw_w2, rw_a0, rw_a2, rw_k_k, rw_k_a, rw_r_k,
     rw_ln_g, rw_ln_b, w_bo_sb, w_bo_rw, w_bo_mem, w_o) = lp
    b, t = x.shape[:2]
    h = _rmsnorm(x, norm_g)
    proj = h @ w_in
    sq, sk, sv, sz, rp, rz, mq, mz, gl = jnp.split(proj, IN_SPLITS, axis=-1)
    sk_h, sv_h = _heads(sk, SB_HEADS), _heads(sv, SB_HEADS)
    o_sb = sb_fn(_heads(sq, SB_HEADS), sk_h, sv_h, sb_bias).reshape(b, t, SB_WIDTH) * jax.nn.silu(sz)
    o_rw, s_new, last_row = _rwkv_branch(rp, prev_row, s0, shift_mu, rw_w0, rw_w2, rw_a0, rw_a2,
                                         rw_k_k, rw_k_a, rw_r_k, rw_ln_g, rw_ln_b)
    o_rw = o_rw * jax.nn.silu(rz)
    o_mem = _mem_attend(_heads(mq, MEM_HEADS), mk, mv).reshape(b, t, MEM_WIDTH) * jax.nn.silu(mz)
    g_sb, g_rw, g_mem = jnp.split(jax.nn.sigmoid(gl), N_BRANCH, axis=-1)
    merged = g_sb * (o_sb @ w_bo_sb) + g_rw * (o_rw @ w_bo_rw) + g_mem * (o_mem @ w_bo_mem)
    return x + merged @ w_o, sk_h, sv_h, s_new, last_row


def setup_inputs(seed: int = 0) -> dict:
    key = jax.random.key(seed)
    ks = jax.random.split(key, 40)
    f32 = jnp.float32
    n_pages = PAST_LEN // PAGE_SIZE
    n_used = DEC_BATCH * n_pages
    n_pool = n_used + max(1, n_used // 4)
    nrm = lambda k, shape, s=1.0: jax.random.normal(k, shape, f32) * s
    page_table = jax.random.permutation(ks[0], n_pool)[:n_used].reshape(DEC_BATCH, n_pages).astype(jnp.int32)
    return {
        "x_prompt": nrm(ks[1], (BATCH, SEQ, D_MODEL)),
        "x_sample": nrm(ks[2], (DEC_BATCH, DEC_SEQ, D_MODEL)),
        "cache_sb_k": nrm(ks[3], (DEPTH, n_pool, PAGE_SIZE, SB_HEADS, SB_HEAD_DIM)),
        "cache_sb_v": nrm(ks[4], (DEPTH, n_pool, PAGE_SIZE, SB_HEADS, SB_HEAD_DIM)),
        "cache_mem_k": nrm(ks[5], (DEPTH, DEC_BATCH, MEM_TOKENS, MEM_HEADS, MEM_HEAD_DIM)),
        "cache_mem_v": nrm(ks[6], (DEPTH, DEC_BATCH, MEM_TOKENS, MEM_HEADS, MEM_HEAD_DIM)),
        "state_wkv": nrm(ks[7], (DEPTH, DEC_BATCH, RW_HEADS, RW_HEAD_DIM, RW_HEAD_DIM), 0.5),
        "state_shift": nrm(ks[8], (DEPTH, DEC_BATCH, RW_SHIFT_WIDTH)),
        "page_table": page_table,
        "mem_prompt": nrm(ks[9], (BATCH, MEM_TOKENS, D_MODEL)),
        "norm_g": 1.0 + nrm(ks[10], (DEPTH, D_MODEL), 0.02),
        "w_in": nrm(ks[11], (DEPTH, D_MODEL, C_IN), D_MODEL ** -0.5),
        "sb_bias": SB_BIAS_INIT + nrm(ks[29], (DEPTH, SB_HEADS), 0.5),
        "shift_mu": jax.random.uniform(ks[12], (DEPTH, RW_SHIFT_WIDTH), f32),
        "rw_w0": jax.random.uniform(ks[13], (DEPTH, RW_WIDTH), f32, -6.0, -0.5),
        "rw_w2": nrm(ks[14], (DEPTH, RW_DECAY_RANK, RW_WIDTH), RW_DECAY_RANK ** -0.5),
        "rw_a0": nrm(ks[15], (DEPTH, RW_WIDTH), 0.1),
        "rw_a2": nrm(ks[16], (DEPTH, RW_A_RANK, RW_WIDTH), RW_A_RANK ** -0.5),
        "rw_k_k": 0.85 + nrm(ks[17], (DEPTH, RW_WIDTH), 0.05),
        "rw_k_a": 1.0 + nrm(ks[18], (DEPTH, RW_WIDTH), 0.05),
        "rw_r_k": nrm(ks[19], (DEPTH, RW_HEADS, RW_HEAD_DIM), 0.1),
        "rw_ln_g": 1.0 + nrm(ks[20], (DEPTH, RW_WIDTH), 0.02),
        "rw_ln_b": nrm(ks[21], (DEPTH, RW_WIDTH), 0.02),
        "mem_norm_g": 1.0 + nrm(ks[22], (DEPTH, D_MODEL), 0.02),
        "w_mem_kv": nrm(ks[23], (DEPTH, D_MODEL, 2 * MEM_WIDTH), D_MODEL ** -0.5),
        "w_bo_sb": nrm(ks[24], (DEPTH, SB_WIDTH, D_MODEL), SB_WIDTH ** -0.5),
        "w_bo_rw": nrm(ks[25], (DEPTH, RW_WIDTH, D_MODEL), RW_WIDTH ** -0.5),
        "w_bo_mem": nrm(ks[26], (DEPTH, MEM_WIDTH, D_MODEL), MEM_WIDTH ** -0.5),
        "w_o": nrm(ks[27], (DEPTH, D_MODEL, D_MODEL), D_MODEL ** -0.5),
        "final_norm_g": 1.0 + nrm(ks[28], (D_MODEL,), 0.02),
    }


def reference(x_prompt, x_sample, cache_sb_k, cache_sb_v, cache_mem_k, cache_mem_v, state_wkv, state_shift,
              page_table, mem_prompt, norm_g, w_in, sb_bias, shift_mu, rw_w0, rw_w2, rw_a0, rw_a2, rw_k_k, rw_k_a,
              rw_r_k, rw_ln_g, rw_ln_b, mem_norm_g, w_mem_kv, w_bo_sb, w_bo_rw, w_bo_mem, w_o, final_norm_g):
    bp, tp = x_prompt.shape[:2]
    hp, hs = x_prompt, x_sample
    sbk_p, sbv_p, mk_p_all, mv_p_all, wkv_p, shift_p = [], [], [], [], [], []
    sbk_s, sbv_s, wkv_s, shift_s = [], [], [], []
    for l in range(DEPTH):
        lp = (norm_g[l], w_in[l], sb_bias[l], shift_mu[l], rw_w0[l], rw_w2[l], rw_a0[l], rw_a2[l], rw_k_k[l],
              rw_k_a[l], rw_r_k[l], rw_ln_g[l], rw_ln_b[l], w_bo_sb[l], w_bo_rw[l], w_bo_mem[l], w_o[l])
        mk_p, mv_p = _mem_kv(mem_prompt, mem_norm_g[l], w_mem_kv[l])
        s0_p = jnp.zeros((bp, RW_HEADS, RW_HEAD_DIM, RW_HEAD_DIM), jnp.float32)
        prev_p = jnp.zeros((bp, RW_SHIFT_WIDTH), x_prompt.dtype)
        hp, kp, vp, sp, lastp = _layer(hp, mk_p, mv_p, s0_p, prev_p, _sb_prompt, lp)
        sbk_p.append(kp.reshape(bp, tp // PAGE_SIZE, PAGE_SIZE, SB_HEADS, SB_HEAD_DIM))
        sbv_p.append(vp.reshape(bp, tp // PAGE_SIZE, PAGE_SIZE, SB_HEADS, SB_HEAD_DIM))
        mk_p_all.append(mk_p)
        mv_p_all.append(mv_p)
        wkv_p.append(sp)
        shift_p.append(lastp)
        sb_fn = functools.partial(_sb_sample, k_pool=cache_sb_k[l], v_pool=cache_sb_v[l], page_table=page_table)
        hs, ks_, vs_, ss, lasts = _layer(hs, cache_mem_k[l], cache_mem_v[l], state_wkv[l], state_shift[l], sb_fn, lp)
        sbk_s.append(ks_)
        sbv_s.append(vs_)
        wkv_s.append(ss)
        shift_s.append(lasts)
    y_prompt = _rmsnorm(hp, final_norm_g)
    y_sample = _rmsnorm(hs, final_norm_g)
    return (y_prompt, y_sample,
            jnp.stack(sbk_p), jnp.stack(sbv_p), jnp.stack(mk_p_all), jnp.stack(mv_p_all),
            jnp.stack(wkv_p), jnp.stack(shift_p),
            jnp.stack(sbk_s), jnp.stack(sbv_s), jnp.stack(wkv_s), jnp.stack(shift_s))
```

```python
import functools

import numpy as np
import jax
import jax.numpy as jnp
from jax import lax
from jax.experimental import pallas as pl
from jax.experimental.pallas import tpu as pltpu

F32 = jnp.float32
BF16 = jnp.bfloat16

D_MODEL = 1024
SB_HEADS = 8
SB_HEAD_DIM = 64
SB_WIDTH = SB_HEADS * SB_HEAD_DIM
RW_HEADS = 8
RW_HEAD_DIM = 64
RW_WIDTH = RW_HEADS * RW_HEAD_DIM
RW_LORA = 64
RW_SHIFT_WIDTH = 3 * RW_WIDTH + 2 * RW_LORA
MEM_HEADS = 4
MEM_HEAD_DIM = 128
MEM_WIDTH = MEM_HEADS * MEM_HEAD_DIM
GN_EPS = 64e-5
RMS_EPS = 1e-6
C_IN = 4 * SB_WIDTH + RW_SHIFT_WIDTH + RW_WIDTH + 2 * MEM_WIDTH + 3 * D_MODEL

COL_GATE = 0
COL_SQ = 3 * D_MODEL
COL_SK = COL_SQ + SB_WIDTH
COL_SV = COL_SK + SB_WIDTH
COL_SZ = COL_SV + SB_WIDTH
COL_RZ = COL_SZ + SB_WIDTH
COL_MQ = COL_RZ + RW_WIDTH
COL_MZ = COL_MQ + MEM_WIDTH
COL_RP = COL_MZ + MEM_WIDTH
assert COL_RP + RW_SHIFT_WIDTH == C_IN and COL_RP % RW_SHIFT_WIDTH == 0

VMEM_LIMIT_BYTES = 56 * 1024 * 1024
MXU_TILE = 256

RW_CHUNK = 64
RW_GROUP = MXU_TILE // RW_CHUNK
RW_NGROUP = RW_HEADS // RW_GROUP
SB_SUB = MXU_TILE


def _cparams(n_grid):
    return pltpu.CompilerParams(dimension_semantics=("arbitrary",) * n_grid,
                                vmem_limit_bytes=VMEM_LIMIT_BYTES)


def _sigmoid(x):
    return 1.0 / (1.0 + jnp.exp(-x))


def _softplus(x):
    return jnp.maximum(x, 0.0) + jnp.log(1.0 + jnp.exp(-jnp.abs(x)))


def _silu(x):
    return x * _sigmoid(x)


def _split_bf16(x):
    hi = x.astype(BF16)
    lo = (x - hi.astype(F32)).astype(BF16)
    return hi, lo


def _dot(a, b):
    return jnp.dot(a, b, preferred_element_type=F32)


def _dot_nt(a, b):
    return lax.dot_general(a, b, (((1,), (1,)), ((), ())), preferred_element_type=F32)


def _dot_tn(a, b):
    return lax.dot_general(a, b, (((0,), (0,)), ((), ())), preferred_element_type=F32)


def _dot_hilo(x, w01):
    hi, lo = _split_bf16(x)
    return _dot(hi, w01) + _dot(lo, w01)


def _norm_proj_kernel(x_ref, g_ref, w_ref, o_ref, h_ref):
    @pl.when(pl.program_id(1) == 0)
    def _():
        x = x_ref[...]
        ms = jnp.mean(x * x, axis=-1, keepdims=True)
        h_ref[...] = (x * lax.rsqrt(ms + RMS_EPS) * g_ref[...]).astype(BF16)

    o_ref[...] = _dot(h_ref[...], w_ref[...])


def _norm_proj(x, g, w, tm, tn, name):
    t, d = x.shape
    n = w.shape[1]
    return pl.pallas_call(
        _norm_proj_kernel,
        grid=(t // tm, n // tn),
        in_specs=[pl.BlockSpec((tm, d), lambda i, j: (i, 0)),
                  pl.BlockSpec((1, d), lambda i, j: (0, 0)),
                  pl.BlockSpec((d, tn), lambda i, j: (0, j))],
        out_specs=pl.BlockSpec((tm, tn), lambda i, j: (i, j)),
        out_shape=jax.ShapeDtypeStruct((t, n), F32),
        scratch_shapes=[pltpu.VMEM((tm, d), BF16)],
        compiler_params=_cparams(2),
        name=name,
    )(x, g.reshape(1, d), w)


def _sb_prompt_kernel(qi_ref, kj_ref, bias_ref, q_ref, kt_ref, v_ref, tri_ref, o_ref, acc_ref, carry_ref,
                      *, bq, bk):
    s = pl.program_id(0)
    qi = qi_ref[s]
    kj = kj_ref[s]

    @pl.when(kj == qi)
    def _():
        acc_ref[...] = jnp.zeros_like(acc_ref)
        carry_ref[...] = jnp.zeros_like(carry_ref)

    def run(masked):
        def head(h, _):
            q = q_ref[h]
            bias = bias_ref[h]
            tri = tri_ref[...]
            for kb in reversed(range(bk // SB_SUB)):
                ks = slice(kb * SB_SUB, (kb + 1) * SB_SUB)
                z = _dot(q, kt_ref[h, :, ks]) + bias
                sp = _softplus(z)
                if masked:
                    row = lax.broadcasted_iota(jnp.int32, (bq, SB_SUB), 0)
                    col = lax.broadcasted_iota(jnp.int32, (bq, SB_SUB), 1) + kb * SB_SUB
                    vis = col < row
                    sp = jnp.where(vis, sp, 0.0)
                hi, lo = _split_bf16(sp)
                within = _dot(hi, tri) + _dot(lo, tri)
                c = carry_ref[h]
                w = jnp.exp(z - sp - within - c)
                if masked:
                    w = jnp.where(vis, w, 0.0)
                acc_ref[h] += _dot(w.astype(BF16), v_ref[h, ks, :])
                carry_ref[h] = c + jnp.sum(sp, axis=1, keepdims=True)
            return 0

        lax.fori_loop(0, SB_HEADS, head, 0)

    @pl.when(kj == qi)
    def _():
        run(True)

    @pl.when(kj < qi)
    def _():
        run(False)

    @pl.when(kj == 0)
    def _():
        o_ref[...] = acc_ref[...]


def _sb_prompt(q, kt, v, bias, bq, bk):
    h, t, d = q.shape
    nq = t // bq
    assert bq == bk
    qi = np.concatenate([np.full(i + 1, i) for i in range(nq)]).astype(np.int32)
    kj = np.concatenate([np.arange(i, -1, -1) for i in range(nq)]).astype(np.int32)
    tri = (np.arange(SB_SUB)[:, None] > np.arange(SB_SUB)[None, :]).astype(np.float32)
    grid_spec = pltpu.PrefetchScalarGridSpec(
        num_scalar_prefetch=2,
        grid=(len(qi),),
        in_specs=[pl.BlockSpec(memory_space=pltpu.SMEM),
                  pl.BlockSpec((h, bq, d), lambda s, qi, kj: (0, qi[s], 0)),
                  pl.BlockSpec((h, d, bk), lambda s, qi, kj: (0, 0, kj[s])),
                  pl.BlockSpec((h, bk, d), lambda s, qi, kj: (0, kj[s], 0)),
                  pl.BlockSpec((SB_SUB, SB_SUB), lambda s, qi, kj: (0, 0))],
        out_specs=pl.BlockSpec((h, bq, d), lambda s, qi, kj: (0, qi[s], 0)),
        scratch_shapes=[pltpu.VMEM((h, bq, d), F32), pltpu.VMEM((h, bq, 1), F32)],
    )
    return pl.pallas_call(
        functools.partial(_sb_prompt_kernel, bq=bq, bk=bk),
        grid_spec=grid_spec,
        out_shape=jax.ShapeDtypeStruct((h, t, d), F32),
        compiler_params=_cparams(1),
        name="sb_prompt",
    )(jnp.asarray(qi), jnp.asarray(kj), bias, q, kt, v, jnp.asarray(tri, BF16))


def _sb_sample_kernel(pt_ref, q_ref, bias_ref, *refs, pp, n_steps):
    k_refs = refs[:pp]
    v_refs = refs[pp:2 * pp]
    tri_ref, o_ref, acc_ref, carry_ref = refs[2 * pp:]
    g = pl.program_id(1)

    @pl.when(g == 0)
    def _():
        acc_ref[...] = jnp.zeros_like(acc_ref)
        carry_ref[...] = jnp.zeros_like(carry_ref)

    q = q_ref[0]
    rowh = lax.broadcasted_iota(jnp.int32, (SB_HEADS, SB_WIDTH), 0)
    laneh = lax.broadcasted_iota(jnp.int32, (SB_HEADS, SB_WIDTH), 1) // SB_HEAD_DIM
    own = rowh == laneh
    qblk = jnp.where(own, q, 0.0).astype(BF16)
    bias = bias_ref[...]
    z = jnp.concatenate([_dot_nt(qblk, k_refs[i][0].astype(BF16)) + bias for i in range(pp)], axis=0)
    sp = _softplus(z)
    within = _dot_hilo(sp, tri_ref[...])
    tot = jnp.sum(sp, axis=1, keepdims=True)
    c = carry_ref[...]
    cs = [None] * pp
    for i in reversed(range(pp)):
        cs[i] = c
        c = c + tot[i * SB_HEADS:(i + 1) * SB_HEADS]
    carry_ref[...] = c
    w = jnp.exp(z - sp - within - jnp.concatenate(cs, axis=0))
    acc = acc_ref[...]
    for i in range(pp):
        acc = acc + _dot(w[i * SB_HEADS:(i + 1) * SB_HEADS].astype(BF16), v_refs[i][0].astype(BF16))
    acc_ref[...] = acc

    @pl.when(g == n_steps - 1)
    def _():
        o_ref[0] = jnp.sum(jnp.where(own, acc, 0.0), axis=0, keepdims=True)


def _sb_sample(q, bias, k_pool, v_pool, page_table, pp):
    b = q.shape[0]
    n_pages = page_table.shape[1]
    page = k_pool.shape[1]
    n_steps = n_pages // pp
    tri = (np.arange(page)[:, None] > np.arange(page)[None, :]).astype(np.float32)

    def page_spec(i):
        return pl.BlockSpec((1, page, SB_WIDTH), lambda bi, g, pt: (pt[bi, (n_steps - 1 - g) * pp + i], 0, 0))

    grid_spec = pltpu.PrefetchScalarGridSpec(
        num_scalar_prefetch=1,
        grid=(b, n_steps),
        in_specs=[pl.BlockSpec((1, 1, SB_WIDTH), lambda bi, g, pt: (bi, 0, 0)),
                  pl.BlockSpec((SB_HEADS, 1), lambda bi, g, pt: (0, 0))]
        + [page_spec(i) for i in range(pp)] + [page_spec(i) for i in range(pp)]
        + [pl.BlockSpec((page, page), lambda bi, g, pt: (0, 0))],
        out_specs=pl.BlockSpec((1, 1, SB_WIDTH), lambda bi, g, pt: (bi, 0, 0)),
        scratch_shapes=[pltpu.VMEM((SB_HEADS, SB_WIDTH), F32), pltpu.VMEM((SB_HEADS, 1), F32)],
    )
    out = pl.pallas_call(
        functools.partial(_sb_sample_kernel, pp=pp, n_steps=n_steps),
        grid_spec=grid_spec,
        out_shape=jax.ShapeDtypeStruct((b, 1, SB_WIDTH), F32),
        compiler_params=_cparams(2),
        name="sb_sample",
    )(page_table, q.reshape(b, 1, SB_WIDTH), bias.reshape(SB_HEADS, 1),
      *([k_pool] * pp), *([v_pool] * pp), jnp.asarray(tri, BF16))
    return out.reshape(b, SB_WIDTH)


def _rw_prep(p, prev, mu, w0, a0, w2a, k_k, k_a, r_k, gsum):
    u = p + mu * (prev - p)
    r = u[:, 0:RW_WIDTH]
    k = u[:, RW_WIDTH:2 * RW_WIDTH]
    v = u[:, 2 * RW_WIDTH:3 * RW_WIDTH]
    x2 = u[:, 3 * RW_WIDTH:]
    lane = lax.broadcasted_iota(jnp.int32, x2.shape, 1)
    x2 = jnp.where(lane < RW_LORA, jnp.tanh(x2), x2)
    d = _dot(x2.astype(BF16), w2a)
    w_log = -_softplus(-(w0 + d[:, :RW_WIDTH])) - 0.5
    lw = -jnp.exp(w_log)
    a = _sigmoid(a0 + d[:, RW_WIDTH:])
    kk = k * k_k
    kk = kk / jnp.maximum(jnp.sqrt(_dot_hilo(kk * kk, gsum)), 1e-12)
    k = k * (1.0 + (a - 1.0) * k_a)
    bonus = _dot_hilo(r * k * r_k, gsum) * v
    return r, lw, k, v, kk, kk * a, bonus


def _neumann_inverse(a):
    n = a.shape[0]
    eye = (lax.broadcasted_iota(jnp.int32, (n, n), 0) == lax.broadcasted_iota(jnp.int32, (n, n), 1)).astype(F32)
    x = eye - a
    p = a
    order = 2
    while order < RW_CHUNK:
        pb = p.astype(BF16)
        p = _dot(pb, pb)
        x = x + _dot(x.astype(BF16), p.astype(BF16))
        order *= 2
    return x


def _rwkv_prompt_kernel(p_ref, prev_ref, s0_ref, mu_ref, w0_ref, a0_ref, w2a_ref, kk_ref, ka_ref, rk_ref,
                        lng_ref, lnb_ref, gsum_ref, lincl_ref, ones_ref,
                        y_ref, sout_ref,
                        s_scr, last_scr, r_scr, lw_scr, k_scr, v_scr, kk_scr, b_scr, y_scr, *, tb):
    i = pl.program_id(0)

    @pl.when(i == 0)
    def _():
        s_scr[...] = s0_ref[...]
        last_scr[...] = prev_ref[...]

    p = p_ref[...]
    row0 = lax.broadcasted_iota(jnp.int32, p.shape, 0) == 0
    prev = jnp.where(row0, last_scr[...], pltpu.roll(p, 1, axis=0))
    last_scr[...] = p[tb - 1:tb, :]
    gsum = gsum_ref[...]
    r, lw, k, v, kk, b, bonus = _rw_prep(p, prev, mu_ref[...], w0_ref[...], a0_ref[...], w2a_ref[...],
                                         kk_ref[...], ka_ref[...], rk_ref[...], gsum)
    r_scr[...] = r
    lw_scr[...] = lw
    k_scr[...] = k
    v_scr[...] = v
    kk_scr[...] = kk
    b_scr[...] = b
    y_ref[...] = bonus

    n = RW_GROUP * RW_CHUNK
    row = lax.broadcasted_iota(jnp.int32, (n, n), 0)
    col = lax.broadcasted_iota(jnp.int32, (n, n), 1)
    same_head = (row // RW_CHUNK) == (col // RW_HEAD_DIM)
    strict = col < row
    incl = col <= row

    def chunk(ci, _):
        rows = pl.ds(pl.multiple_of(ci * RW_CHUNK, RW_CHUNK), RW_CHUNK)
        lwc = lw_scr[rows, :]
        lhi, llo = _split_bf16(lwc)
        lincl = lincl_ref[...]
        cum = _dot(lincl, lhi) + _dot(lincl, llo)
        cum_end = cum[RW_CHUNK - 1:RW_CHUNK, :]
        ones = ones_ref[...]
        g_end_col = jnp.exp(_dot_tn(lhi, ones) + _dot_tn(llo, ones))
        g_in = jnp.exp(cum)
        g_inv = jnp.exp(-cum)
        g_out = jnp.exp(cum_end - cum)
        rc = r_scr[rows, :] * g_in
        kkc = kk_scr[rows, :] * jnp.exp(cum - lwc)
        bc = b_scr[rows, :]
        kc = k_scr[rows, :]
        vc = v_scr[rows, :]
        bt = bc * g_inv
        kt = kc * g_inv
        bh = bc * g_out
        kh = kc * g_out
        for g in range(RW_NGROUP):
            sl = slice(g * n, (g + 1) * n)

            def bdiag(x):
                return jnp.where(same_head, jnp.concatenate([x[:, sl]] * RW_GROUP, axis=0), 0.0).astype(BF16)

            kk_b, r_b, bt_b, kt_b, bh_b, kh_b, v_b = (bdiag(x) for x in (kkc, rc, bt, kt, bh, kh, vc))
            a_b = jnp.where(strict, _dot_nt(kk_b, bt_b), 0.0)
            a_k = jnp.where(strict, _dot_nt(kk_b, kt_b), 0.0).astype(BF16)
            p_b = jnp.where(incl, _dot_nt(r_b, bt_b), 0.0).astype(BF16)
            p_k = jnp.where(incl, _dot_nt(r_b, kt_b), 0.0).astype(BF16)
            t_inv = _neumann_inverse(a_b).astype(BF16)
            s_old = s_scr[g]
            s_bf = s_old.astype(BF16)
            u = -_dot(t_inv, (_dot(kk_b, s_bf) + _dot(a_k, v_b)).astype(BF16))
            u_bf = u.astype(BF16)
            y = _dot(r_b, s_bf) + _dot(p_b, u_bf) + _dot(p_k, v_b)
            g_end = g_end_col[sl, :]
            s_scr[g] = (jnp.concatenate([g_end] * (n // 128), axis=1) * s_old
                        + _dot_tn(bh_b, u_bf) + _dot_tn(kh_b, v_b))
            yg = y[0:RW_CHUNK]
            for hh in range(1, RW_GROUP):
                yg = yg + y[hh * RW_CHUNK:(hh + 1) * RW_CHUNK]
            y_scr[rows, sl] = yg
        return 0

    lax.fori_loop(0, tb // RW_CHUNK, chunk, 0)

    y = y_scr[...]
    inv_n = 1.0 / RW_HEAD_DIM
    mean = _dot_hilo(y, gsum) * inv_n
    d = y - mean
    var = _dot_hilo(d * d, gsum) * inv_n
    y_ref[...] = d * lax.rsqrt(var + GN_EPS) * lng_ref[...] + lnb_ref[...] + y_ref[...]

    @pl.when(i == pl.num_programs(0) - 1)
    def _():
        sout_ref[...] = s_scr[...]


def _rw_consts():
    hd = np.arange(RW_WIDTH) // RW_HEAD_DIM
    gsum = (hd[:, None] == hd[None, :]).astype(np.float32)
    lincl = (np.arange(RW_CHUNK)[:, None] >= np.arange(RW_CHUNK)[None, :]).astype(np.float32)
    ones = np.ones((RW_CHUNK, 128), np.float32)
    return jnp.asarray(gsum, BF16), jnp.asarray(lincl, BF16), jnp.asarray(ones, BF16)


def _rw_params(lp):
    row = lambda a: a.reshape(1, -1)
    z = jnp.zeros((RW_LORA, RW_WIDTH), F32)
    w2a = jnp.concatenate([jnp.concatenate([lp["rw_w2"], z], axis=1),
                           jnp.concatenate([z, lp["rw_a2"]], axis=1)], axis=0).astype(BF16)
    return dict(mu=row(lp["shift_mu"]), w0=row(lp["rw_w0"]), a0=row(lp["rw_a0"]), w2a=w2a,
                k_k=row(lp["rw_k_k"]), k_a=row(lp["rw_k_a"]), r_k=row(lp["rw_r_k"]),
                ln_g=row(lp["rw_ln_g"]), ln_b=row(lp["rw_ln_b"]))


def _rwkv_prompt(proj, prev_row, s0_bd, rwp, tb):
    t = proj.shape[0]
    gsum, lincl, ones = _rw_consts()
    n = RW_GROUP * RW_CHUNK
    const = lambda shape: pl.BlockSpec(shape, lambda i: (0,) * len(shape))
    vec = const((1, RW_WIDTH))
    return pl.pallas_call(
        functools.partial(_rwkv_prompt_kernel, tb=tb),
        grid=(t // tb,),
        in_specs=[pl.BlockSpec((tb, RW_SHIFT_WIDTH), lambda i: (i, COL_RP // RW_SHIFT_WIDTH)),
                  const((1, RW_SHIFT_WIDTH)), const((RW_NGROUP, n, n)), const((1, RW_SHIFT_WIDTH)),
                  vec, vec, const((2 * RW_LORA, 2 * RW_WIDTH)), vec, vec, vec, vec, vec,
                  const((RW_WIDTH, RW_WIDTH)), const((RW_CHUNK, RW_CHUNK)), const((RW_CHUNK, 128))],
        out_specs=[pl.BlockSpec((tb, RW_WIDTH), lambda i: (i, 0)), const((RW_NGROUP, n, n))],
        out_shape=[jax.ShapeDtypeStruct((t, RW_WIDTH), F32), jax.ShapeDtypeStruct((RW_NGROUP, n, n), F32)],
        scratch_shapes=[pltpu.VMEM((RW_NGROUP, n, n), F32), pltpu.VMEM((1, RW_SHIFT_WIDTH), F32)]
        + [pltpu.VMEM((tb, RW_WIDTH), F32)] * 7,
        compiler_params=_cparams(1),
        name="rwkv_prompt",
    )(proj, prev_row, s0_bd, rwp["mu"], rwp["w0"], rwp["a0"], rwp["w2a"], rwp["k_k"], rwp["k_a"], rwp["r_k"],
      rwp["ln_g"], rwp["ln_b"], gsum, lincl, ones)


def _state_to_bd(s):
    st = jnp.swapaxes(s, -1, -2).reshape(RW_NGROUP, RW_GROUP, RW_HEAD_DIM, RW_HEAD_DIM)
    eye = jnp.eye(RW_GROUP, dtype=s.dtype)
    bd = st[:, :, :, None, :] * eye[None, :, None, :, None]
    n = RW_GROUP * RW_HEAD_DIM
    return bd.reshape(RW_NGROUP, n, n)


def _bd_to_state(bd):
    n = RW_GROUP * RW_HEAD_DIM
    b5 = bd.reshape(RW_NGROUP, RW_GROUP, RW_HEAD_DIM, RW_GROUP, RW_HEAD_DIM)
    idx = jnp.arange(RW_GROUP)
    blocks = b5[:, idx, :, idx, :]
    blocks = jnp.swapaxes(blocks, 0, 1).reshape(RW_HEADS, RW_HEAD_DIM, RW_HEAD_DIM)
    return jnp.swapaxes(blocks, -1, -2)


def _rwkv_sample_prep_kernel(p_ref, prev_ref, mu_ref, w0_ref, a0_ref, w2a_ref, kk_ref, ka_ref, rk_ref, gsum_ref,
                             r_ref, w_ref, k_ref, v_ref, kko_ref, b_ref, bonus_ref):
    r, lw, k, v, kk, b, bonus = _rw_prep(p_ref[...], prev_ref[...], mu_ref[...], w0_ref[...], a0_ref[...],
                                         w2a_ref[...], kk_ref[...], ka_ref[...], rk_ref[...], gsum_ref[...])
    r_ref[...] = r
    w_ref[...] = jnp.exp(lw)
    k_ref[...] = k
    v_ref[...] = v
    kko_ref[...] = kk
    b_ref[...] = b
    bonus_ref[...] = bonus


def _rwkv_sample_state_kernel(s_ref, r_ref, w_ref, k_ref, kk_ref, b_ref, v_ref, bonus_ref, lng_ref, lnb_ref,
                              y_ref, so_ref):
    s = s_ref[0]
    sa = jnp.sum(s * kk_ref[0], axis=-1, keepdims=True)
    s = s * w_ref[0] - sa * b_ref[0] + v_ref[0] * k_ref[0]
    so_ref[0] = s
    y = jnp.sum(s * r_ref[0], axis=-1, keepdims=True)
    mean = jnp.mean(y, axis=1, keepdims=True)
    d = y - mean
    var = jnp.mean(d * d, axis=1, keepdims=True)
    y_ref[0] = d * lax.rsqrt(var + GN_EPS) * lng_ref[...] + lnb_ref[...] + bonus_ref[0]


def _rwkv_sample(proj, prev_rows, state, rwp):
    b = proj.shape[0]
    gsum, _, _ = _rw_consts()
    full = lambda shape: pl.BlockSpec(shape, lambda i: (0,) * len(shape))
    vec = full((1, RW_WIDTH))
    outs = pl.pallas_call(
        _rwkv_sample_prep_kernel,
        grid=(1,),
        in_specs=[pl.BlockSpec((b, RW_SHIFT_WIDTH), lambda i: (0, COL_RP // RW_SHIFT_WIDTH)),
                  full((b, RW_SHIFT_WIDTH)), full((1, RW_SHIFT_WIDTH)), vec, vec,
                  full((2 * RW_LORA, 2 * RW_WIDTH)), vec, vec, vec, full((RW_WIDTH, RW_WIDTH))],
        out_specs=[full((b, RW_WIDTH))] * 7,
        out_shape=[jax.ShapeDtypeStruct((b, RW_WIDTH), F32)] * 7,
        compiler_params=_cparams(1),
        name="rwkv_sample_prep",
    )(proj, prev_rows, rwp["mu"], rwp["w0"], rwp["a0"], rwp["w2a"], rwp["k_k"], rwp["k_a"], rwp["r_k"], gsum)
    r, w, k, v, kk, bb, bonus = outs
    as_row = lambda a: a.reshape(b, RW_HEADS, 1, RW_HEAD_DIM)
    as_col = lambda a: a.reshape(-1, RW_HEADS, RW_HEAD_DIM, 1)
    row_spec = pl.BlockSpec((1, RW_HEADS, 1, RW_HEAD_DIM), lambda i: (i, 0, 0, 0))
    col_spec = pl.BlockSpec((1, RW_HEADS, RW_HEAD_DIM, 1), lambda i: (i, 0, 0, 0))
    par_spec = pl.BlockSpec((RW_HEADS, RW_HEAD_DIM, 1), lambda i: (0, 0, 0))
    s_spec = pl.BlockSpec((1, RW_HEADS, RW_HEAD_DIM, RW_HEAD_DIM), lambda i: (i, 0, 0, 0))
    y, s_new = pl.pallas_call(
        _rwkv_sample_state_kernel,
        grid=(b,),
        in_specs=[s_spec, row_spec, row_spec, row_spec, row_spec, row_spec, col_spec, col_spec, par_spec, par_spec],
        out_specs=[col_spec, s_spec],
        out_shape=[jax.ShapeDtypeStruct((b, RW_HEADS, RW_HEAD_DIM, 1), F32),
                   jax.ShapeDtypeStruct(state.shape, F32)],
        compiler_params=_cparams(1),
        name="rwkv_sample_state",
    )(state, as_row(r), as_row(w), as_row(k), as_row(kk), as_row(bb), as_col(v), as_col(bonus),
      rwp["ln_g"].reshape(RW_HEADS, RW_HEAD_DIM, 1), rwp["ln_b"].reshape(RW_HEADS, RW_HEAD_DIM, 1))
    return y.reshape(b, RW_WIDTH), s_new


def _mem_prompt_kernel(q_ref, mk_ref, mv_ref, o_ref):
    scale = MEM_HEAD_DIM ** -0.5
    for h in range(MEM_HEADS):
        sl = slice(h * MEM_HEAD_DIM, (h + 1) * MEM_HEAD_DIM)
        s = _dot_nt(q_ref[:, sl].astype(BF16), mk_ref[:, sl].astype(BF16)) * scale
        p = jnp.exp(s - jnp.max(s, axis=1, keepdims=True))
        l = jnp.sum(p, axis=1, keepdims=True)
        o_ref[:, sl] = _dot(p.astype(BF16), mv_ref[:, sl].astype(BF16)) / l


def _mem_prompt(proj, mk, mv, tm):
    t = proj.shape[0]
    m = mk.shape[0]
    return pl.pallas_call(
        _mem_prompt_kernel,
        grid=(t // tm,),
        in_specs=[pl.BlockSpec((tm, MEM_WIDTH), lambda i: (i, COL_MQ // MEM_WIDTH)),
                  pl.BlockSpec((m, MEM_WIDTH), lambda i: (0, 0)),
                  pl.BlockSpec((m, MEM_WIDTH), lambda i: (0, 0))],
        out_specs=pl.BlockSpec((tm, MEM_WIDTH), lambda i: (i, 0)),
        out_shape=jax.ShapeDtypeStruct((t, MEM_WIDTH), F32),
        compiler_params=_cparams(1),
        name="mem_prompt",
    )(proj, mk, mv)


def _mem_sample_kernel(q_ref, mk_ref, mv_ref, o_ref):
    scale = MEM_HEAD_DIM ** -0.5
    q = q_ref[0]
    rowh = lax.broadcasted_iota(jnp.int32, (8, MEM_WIDTH), 0)
    laneh = lax.broadcasted_iota(jnp.int32, (8, MEM_WIDTH), 1) // MEM_HEAD_DIM
    own = rowh == laneh
    qblk = jnp.where(own, q, 0.0).astype(BF16)
    s = _dot_nt(qblk, mk_ref[0].astype(BF16)) * scale
    p = jnp.exp(s - jnp.max(s, axis=1, keepdims=True))
    l = jnp.sum(p, axis=1, keepdims=True)
    o = _dot(p.astype(BF16), mv_ref[0].astype(BF16)) / l
    o_ref[0] = jnp.sum(jnp.where(own, o, 0.0), axis=0, keepdims=True)


def _mem_sample(q, mk, mv):
    b, m, _ = mk.shape
    q_spec = pl.BlockSpec((1, 1, MEM_WIDTH), lambda i: (i, 0, 0))
    kv_spec = pl.BlockSpec((1, m, MEM_WIDTH), lambda i: (i, 0, 0))
    out = pl.pallas_call(
        _mem_sample_kernel,
        grid=(b,),
        in_specs=[q_spec, kv_spec, kv_spec],
        out_specs=q_spec,
        out_shape=jax.ShapeDtypeStruct((b, 1, MEM_WIDTH), F32),
        compiler_params=_cparams(1),
        name="mem_sample",
    )(q.reshape(b, 1, MEM_WIDTH), mk, mv)
    return out.reshape(b, MEM_WIDTH)


def _merge_kernel(x_ref, gl_ref, osb_ref, sz_ref, orw_ref, rz_ref, omem_ref, mz_ref,
                  wsb_ref, wrw_ref, wmem_ref, wo_ref, fg_ref, o_ref, *, final):
    a_sb = (osb_ref[...] * _silu(sz_ref[...])).astype(BF16)
    a_rw = (orw_ref[...] * _silu(rz_ref[...])).astype(BF16)
    a_mem = (omem_ref[...] * _silu(mz_ref[...])).astype(BF16)
    merged = (_sigmoid(gl_ref[:, 0:D_MODEL]) * _dot(a_sb, wsb_ref[...])
              + _sigmoid(gl_ref[:, D_MODEL:2 * D_MODEL]) * _dot(a_rw, wrw_ref[...])
              + _sigmoid(gl_ref[:, 2 * D_MODEL:3 * D_MODEL]) * _dot(a_mem, wmem_ref[...]))
    y = x_ref[...] + _dot(merged.astype(BF16), wo_ref[...])
    if final:
        ms = jnp.mean(y * y, axis=-1, keepdims=True)
        y = y * lax.rsqrt(ms + RMS_EPS) * fg_ref[...]
    o_ref[...] = y


def _merge(x, proj, o_sb, o_rw, o_mem, w_sb, w_rw, w_mem, w_o, final_g, final, tm, name):
    t = x.shape[0]
    col = lambda c: pl.BlockSpec((tm, SB_WIDTH), lambda i: (i, c // SB_WIDTH))
    act = pl.BlockSpec((tm, SB_WIDTH), lambda i: (i, 0))
    wspec = pl.BlockSpec((SB_WIDTH, D_MODEL), lambda i: (0, 0))
    return pl.pallas_call(
        functools.partial(_merge_kernel, final=final),
        grid=(t // tm,),
        in_specs=[pl.BlockSpec((tm, D_MODEL), lambda i: (i, 0)),
                  pl.BlockSpec((tm, 3 * D_MODEL), lambda i: (i, 0)),
                  act, col(COL_SZ), act, col(COL_RZ), act, col(COL_MZ),
                  wspec, wspec, wspec, pl.BlockSpec((D_MODEL, D_MODEL), lambda i: (0, 0)),
                  pl.BlockSpec((1, D_MODEL), lambda i: (0, 0))],
        out_specs=pl.BlockSpec((tm, D_MODEL), lambda i: (i, 0)),
        out_shape=jax.ShapeDtypeStruct((t, D_MODEL), F32),
        compiler_params=_cparams(1),
        name=name,
    )(x, proj, o_sb, proj, o_rw, proj, o_mem, proj, w_sb, w_rw, w_mem, w_o, final_g.reshape(1, D_MODEL))


def _permute_w_in(w_in):
    sq, sk, sv, sz, rp, rz, mq, mz, gl = jnp.split(
        w_in, np.cumsum([SB_WIDTH] * 4 + [RW_SHIFT_WIDTH, RW_WIDTH, MEM_WIDTH, MEM_WIDTH]).tolist(), axis=1)
    return jnp.concatenate([gl, sq, sk, sv, sz, rz, mq, mz, rp], axis=1).astype(BF16)


def _heads_major(x, scale=None):
    t = x.shape[0]
    if scale is not None:
        x = x * scale
    return jnp.transpose(x.reshape(t, SB_HEADS, SB_HEAD_DIM), (1, 0, 2)).astype(BF16)


def kernel(x_prompt, x_sample, cache_sb_k, cache_sb_v, cache_mem_k, cache_mem_v, state_wkv, state_shift, page_table, mem_prompt, norm_g, w_in, sb_bias, shift_mu, rw_w0, rw_w2, rw_a0, rw_a2, rw_k_k, rw_k_a, rw_r_k, rw_ln_g, rw_ln_b, mem_norm_g, w_mem_kv, w_bo_sb, w_bo_rw, w_bo_mem, w_o, final_norm_g):
    depth = w_in.shape[0]
    bp, tp, _ = x_prompt.shape
    bs, ts, _ = x_sample.shape
    assert bp == 1 and ts == 1
    n_pool, page = cache_sb_k.shape[1:3]
    n_mem = mem_prompt.shape[1]
    sb_scale = SB_HEAD_DIM ** -0.5

    hp = x_prompt.reshape(tp, D_MODEL)
    hs = x_sample.reshape(bs, D_MODEL)
    mem = mem_prompt.reshape(n_mem, D_MODEL)
    outs = {k: [] for k in ("sbk_p", "sbv_p", "mk_p", "mv_p", "wkv_p", "shift_p", "sbk_s", "sbv_s", "wkv_s", "shift_s")}
    for l in range(depth):
        last = l == depth - 1
        lp = dict(shift_mu=shift_mu[l], rw_w0=rw_w0[l], rw_w2=rw_w2[l], rw_a0=rw_a0[l], rw_a2=rw_a2[l],
                  rw_k_k=rw_k_k[l], rw_k_a=rw_k_a[l], rw_r_k=rw_r_k[l], rw_ln_g=rw_ln_g[l], rw_ln_b=rw_ln_b[l])
        rwp = _rw_params(lp)
        w_perm = _permute_w_in(w_in[l])
        w_sb, w_rw, w_mem, w_out = (w.astype(BF16) for w in (w_bo_sb[l], w_bo_rw[l], w_bo_mem[l], w_o[l]))

        proj = _norm_proj(hp, norm_g[l], w_perm, 1024, 640, "proj_prompt")
        kv = _norm_proj(mem, mem_norm_g[l], w_mem_kv[l].astype(BF16), n_mem, MEM_WIDTH, "mem_kv")
        mk, mv = kv[:, :MEM_WIDTH], kv[:, MEM_WIDTH:]
        sk, sv = proj[:, COL_SK:COL_SK + SB_WIDTH], proj[:, COL_SV:COL_SV + SB_WIDTH]
        q_hm = _heads_major(proj[:, COL_SQ:COL_SQ + SB_WIDTH], sb_scale)
        kt_hm = jnp.transpose(sk.reshape(tp, SB_HEADS, SB_HEAD_DIM), (1, 2, 0)).astype(BF16)
        o_sb = _sb_prompt(q_hm, kt_hm, _heads_major(sv), sb_bias[l], 512, 512)
        o_sb = jnp.transpose(o_sb, (1, 0, 2)).reshape(tp, SB_WIDTH)
        s0 = _state_to_bd(jnp.zeros((RW_HEADS, RW_HEAD_DIM, RW_HEAD_DIM), F32))
        o_rw, s_bd = _rwkv_prompt(proj, jnp.zeros((1, RW_SHIFT_WIDTH), F32), s0, rwp, 512)
        o_mem = _mem_prompt(proj, mk, mv, 512)
        hp = _merge(hp, proj, o_sb, o_rw, o_mem, w_sb, w_rw, w_mem, w_out, final_norm_g, last, 256, "merge_prompt")
        outs["sbk_p"].append(sk.reshape(bp, tp // page, page, SB_HEADS, SB_HEAD_DIM))
        outs["sbv_p"].append(sv.reshape(bp, tp // page, page, SB_HEADS, SB_HEAD_DIM))
        outs["mk_p"].append(mk.reshape(bp, n_mem, MEM_HEADS, MEM_HEAD_DIM))
        outs["mv_p"].append(mv.reshape(bp, n_mem, MEM_HEADS, MEM_HEAD_DIM))
        outs["wkv_p"].append(_bd_to_state(s_bd).reshape(bp, RW_HEADS, RW_HEAD_DIM, RW_HEAD_DIM))
        outs["shift_p"].append(proj[tp - 1:tp, COL_RP:])

        proj_s = _norm_proj(hs, norm_g[l], w_perm, bs, 640, "proj_sample")
        sk_s, sv_s = proj_s[:, COL_SK:COL_SK + SB_WIDTH], proj_s[:, COL_SV:COL_SV + SB_WIDTH]
        o_sb_s = _sb_sample(proj_s[:, COL_SQ:COL_SQ + SB_WIDTH] * sb_scale, sb_bias[l],
                            cache_sb_k[l].reshape(n_pool, page, SB_WIDTH),
                            cache_sb_v[l].reshape(n_pool, page, SB_WIDTH), page_table, 8)
        o_rw_s, s_new = _rwkv_sample(proj_s, state_shift[l], state_wkv[l], rwp)
        o_mem_s = _mem_sample(proj_s[:, COL_MQ:COL_MQ + MEM_WIDTH],
                              cache_mem_k[l].reshape(bs, n_mem, MEM_WIDTH),
                              cache_mem_v[l].reshape(bs, n_mem, MEM_WIDTH))
        hs = _merge(hs, proj_s, o_sb_s, o_rw_s, o_mem_s, w_sb, w_rw, w_mem, w_out, final_norm_g, last, bs,
                    "merge_sample")
        outs["sbk_s"].append(sk_s.reshape(bs, ts, SB_HEADS, SB_HEAD_DIM))
        outs["sbv_s"].append(sv_s.reshape(bs, ts, SB_HEADS, SB_HEAD_DIM))
        outs["wkv_s"].append(s_new)
        outs["shift_s"].append(proj_s[:, COL_RP:])

    st = {k: jnp.stack(v) for k, v in outs.items()}
    return (hp.reshape(bp, tp, D_MODEL), hs.reshape(bs, ts, D_MODEL),
            st["sbk_p"], st["sbv_p"], st["mk_p"], st["mv_p"], st["wkv_p"], st["shift_p"],
            st["sbk_s"], st["sbv_s"], st["wkv_s"], st["shift_s"])
```

```python
import functools

import numpy as np
import jax
import jax.numpy as jnp
from jax import lax
from jax.experimental import pallas as pl
from jax.experimental.pallas import tpu as pltpu

F32 = jnp.float32
BF16 = jnp.bfloat16

D_MODEL = 1024
SB_HEADS = 8
SB_HEAD_DIM = 64
SB_WIDTH = SB_HEADS * SB_HEAD_DIM
RW_HEADS = 8
RW_HEAD_DIM = 64
RW_WIDTH = RW_HEADS * RW_HEAD_DIM
RW_LORA = 64
RW_SHIFT_WIDTH = 3 * RW_WIDTH + 2 * RW_LORA
MEM_HEADS = 4
MEM_HEAD_DIM = 128
MEM_WIDTH = MEM_HEADS * MEM_HEAD_DIM
GN_EPS = 64e-5
RMS_EPS = 1e-6
C_IN = 4 * SB_WIDTH + RW_SHIFT_WIDTH + RW_WIDTH + 2 * MEM_WIDTH + 3 * D_MODEL

COL_GATE = 0
COL_SQ = 3 * D_MODEL
COL_SK = COL_SQ + SB_WIDTH
COL_SV = COL_SK + SB_WIDTH
COL_SZ = COL_SV + SB_WIDTH
COL_RZ = COL_SZ + SB_WIDTH
COL_MQ = COL_RZ + RW_WIDTH
COL_MZ = COL_MQ + MEM_WIDTH
COL_RP = COL_MZ + MEM_WIDTH
assert COL_RP + RW_SHIFT_WIDTH == C_IN and COL_RP % RW_SHIFT_WIDTH == 0

VMEM_LIMIT_BYTES = 56 * 1024 * 1024
MXU_TILE = 256

RW_CHUNK = 64
RW_GROUP = MXU_TILE // RW_CHUNK
RW_NGROUP = RW_HEADS // RW_GROUP
SB_SUB = MXU_TILE


def _cparams(n_grid):
    return pltpu.CompilerParams(dimension_semantics=("arbitrary",) * n_grid,
                                vmem_limit_bytes=VMEM_LIMIT_BYTES)


def _sigmoid(x):
    return 1.0 / (1.0 + jnp.exp(-x))


def _softplus(x):
    return jnp.maximum(x, 0.0) + jnp.log(1.0 + jnp.exp(-jnp.abs(x)))


def _silu(x):
    return x * _sigmoid(x)


def _split_bf16(x):
    hi = x.astype(BF16)
    lo = (x - hi.astype(F32)).astype(BF16)
    return hi, lo


def _dot(a, b):
    return jnp.dot(a, b, preferred_element_type=F32)


def _dot_nt(a, b):
    return lax.dot_general(a, b, (((1,), (1,)), ((), ())), preferred_element_type=F32)


def _dot_tn(a, b):
    return lax.dot_general(a, b, (((0,), (0,)), ((), ())), preferred_element_type=F32)


def _dot_hilo(x, w01):
    hi, lo = _split_bf16(x)
    return _dot(hi, w01) + _dot(lo, w01)


def _norm_proj_kernel(x_ref, g_ref, w_ref, o_ref, h_ref):
    @pl.when(pl.program_id(1) == 0)
    def _():
        x = x_ref[...]
        ms = jnp.mean(x * x, axis=-1, keepdims=True)
        h_ref[...] = (x * lax.rsqrt(ms + RMS_EPS) * g_ref[...]).astype(BF16)

    o_ref[...] = _dot(h_ref[...], w_ref[...])


def _norm_proj(x, g, w, tm, tn, name):
    t, d = x.shape
    n = w.shape[1]
    return pl.pallas_call(
        _norm_proj_kernel,
        grid=(t // tm, n // tn),
        in_specs=[pl.BlockSpec((tm, d), lambda i, j: (i, 0)),
                  pl.BlockSpec((1, d), lambda i, j: (0, 0)),
                  pl.BlockSpec((d, tn), lambda i, j: (0, j))],
        out_specs=pl.BlockSpec((tm, tn), lambda i, j: (i, j)),
        out_shape=jax.ShapeDtypeStruct((t, n), F32),
        scratch_shapes=[pltpu.VMEM((tm, d), BF16)],
        compiler_params=_cparams(2),
        name=name,
    )(x, g.reshape(1, d), w)


def _sb_prompt_kernel(qi_ref, kj_ref, bias_ref, q_ref, kt_ref, v_ref, tri_ref, o_ref, acc_ref, carry_ref,
                      *, bq, bk):
    s = pl.program_id(0)
    qi = qi_ref[s]
    kj = kj_ref[s]
    diag = kj == (qi * bq) // bk

    @pl.when(diag)
    def _():
        acc_ref[...] = jnp.zeros_like(acc_ref)
        carry_ref[...] = jnp.zeros_like(carry_ref)

    def run(masked):
        def head(h, _):
            q = q_ref[h]
            bias = bias_ref[h]
            tri = tri_ref[...]
            c = carry_ref[h]
            acc = acc_ref[h]
            for kb in reversed(range(bk // SB_SUB)):
                ks = slice(kb * SB_SUB, (kb + 1) * SB_SUB)
                z = _dot(q, kt_ref[h, :, ks]) + bias
                neg_abs = pltpu.bitcast(pltpu.bitcast(z, jnp.uint32) | jnp.uint32(0x80000000), F32)
                sp = jnp.maximum(z, 0.0) + jnp.log(1.0 + jnp.exp(neg_abs))
                if masked:
                    row = lax.broadcasted_iota(jnp.int32, (bq, SB_SUB), 0) + qi * bq
                    col = lax.broadcasted_iota(jnp.int32, (bq, SB_SUB), 1) + (kj * bk + kb * SB_SUB)
                    vis = col < row
                    sp = jnp.where(vis, sp, 0.0)
                within = _dot(sp.astype(BF16), tri)
                w = jnp.exp(z - sp - within - c)
                if masked:
                    w = jnp.where(vis, w, 0.0)
                acc = acc + _dot(w.astype(BF16), v_ref[h, ks, :])
                c = c + jnp.sum(sp, axis=1, keepdims=True)
            carry_ref[h] = c
            acc_ref[h] = acc
            return 0

        lax.fori_loop(0, SB_HEADS, head, 0)

    @pl.when(diag)
    def _():
        run(True)

    @pl.when(jnp.logical_not(diag))
    def _():
        run(False)

    @pl.when(kj == 0)
    def _():
        o_ref[...] = acc_ref[...]


def _sb_prompt(q, kt, v, bias, bq, bk):
    h, t, d = q.shape
    nq = t // bq
    assert bk % bq == 0 and t % bk == 0
    first = [(i * bq) // bk for i in range(nq)]
    qi = np.concatenate([np.full(first[i] + 1, i) for i in range(nq)]).astype(np.int32)
    kj = np.concatenate([np.arange(first[i], -1, -1) for i in range(nq)]).astype(np.int32)
    tri = (np.arange(SB_SUB)[:, None] > np.arange(SB_SUB)[None, :]).astype(np.float32)
    grid_spec = pltpu.PrefetchScalarGridSpec(
        num_scalar_prefetch=2,
        grid=(len(qi),),
        in_specs=[pl.BlockSpec(memory_space=pltpu.SMEM),
                  pl.BlockSpec((h, bq, d), lambda s, qi, kj: (0, qi[s], 0)),
                  pl.BlockSpec((h, d, bk), lambda s, qi, kj: (0, 0, kj[s])),
                  pl.BlockSpec((h, bk, d), lambda s, qi, kj: (0, kj[s], 0)),
                  pl.BlockSpec((SB_SUB, SB_SUB), lambda s, qi, kj: (0, 0))],
        out_specs=pl.BlockSpec((h, bq, d), lambda s, qi, kj: (0, qi[s], 0)),
        scratch_shapes=[pltpu.VMEM((h, bq, d), F32), pltpu.VMEM((h, bq, 1), F32)],
    )
    return pl.pallas_call(
        functools.partial(_sb_prompt_kernel, bq=bq, bk=bk),
        grid_spec=grid_spec,
        out_shape=jax.ShapeDtypeStruct((h, t, d), F32),
        compiler_params=_cparams(1),
        name="sb_prompt",
    )(jnp.asarray(qi), jnp.asarray(kj), bias, q, kt, v, jnp.asarray(tri, BF16))


def _sb_sample_kernel(pt_ref, q_ref, bias_ref, *refs, pp, n_steps):
    k_refs = refs[:pp]
    v_refs = refs[pp:2 * pp]
    tri_ref, o_ref, acc_ref, carry_ref, z_scr, w_scr = refs[2 * pp:]
    g = pl.program_id(1)

    @pl.when(g == 0)
    def _():
        acc_ref[...] = jnp.zeros_like(acc_ref)
        carry_ref[...] = jnp.zeros_like(carry_ref)

    q = q_ref[0]
    for i in range(pp):
        for h in range(SB_HEADS):
            z_scr[i * SB_HEADS + h:i * SB_HEADS + h + 1, :] = jnp.sum(k_refs[i][0, 0, h] * q[h], axis=0, keepdims=True)
    z = z_scr[...] + jnp.concatenate([bias_ref[...]] * pp, axis=0)
    sp = _softplus(z)
    within = _dot_hilo(sp, tri_ref[...])
    tot = jnp.sum(sp, axis=1, keepdims=True)
    c = carry_ref[...]
    cs = [None] * pp
    for i in reversed(range(pp)):
        cs[i] = c
        c = c + tot[i * SB_HEADS:(i + 1) * SB_HEADS]
    carry_ref[...] = c
    w_scr[...] = jnp.exp(z - sp - within - jnp.concatenate(cs, axis=0))
    for h in range(SB_HEADS):
        a = acc_ref[h]
        for i in range(pp):
            a = a + v_refs[i][0, 0, h] * w_scr[i * SB_HEADS + h:i * SB_HEADS + h + 1, :]
        acc_ref[h] = a

    @pl.when(g == n_steps - 1)
    def _():
        o_ref[0] = jnp.sum(acc_ref[...], axis=2, keepdims=True)


def _sb_sample(q, bias, k_pool, v_pool, layer, page_table, pp):
    b = q.shape[0]
    n_pages = page_table.shape[1]
    page = k_pool.shape[-1]
    n_steps = n_pages // pp
    tri = (np.arange(page)[:, None] > np.arange(page)[None, :]).astype(np.float32)

    def page_spec(i):
        return pl.BlockSpec((1, 1, SB_HEADS, SB_HEAD_DIM, page),
                            lambda bi, g, pt: (layer, pt[bi, (n_steps - 1 - g) * pp + i], 0, 0, 0))

    q_spec = pl.BlockSpec((1, SB_HEADS, SB_HEAD_DIM, 1), lambda bi, g, pt: (bi, 0, 0, 0))
    grid_spec = pltpu.PrefetchScalarGridSpec(
        num_scalar_prefetch=1,
        grid=(b, n_steps),
        in_specs=[q_spec, pl.BlockSpec((SB_HEADS, 1), lambda bi, g, pt: (0, 0))]
        + [page_spec(i) for i in range(pp)] + [page_spec(i) for i in range(pp)]
        + [pl.BlockSpec((page, page), lambda bi, g, pt: (0, 0))],
        out_specs=q_spec,
        scratch_shapes=[pltpu.VMEM((SB_HEADS, SB_HEAD_DIM, page), F32), pltpu.VMEM((SB_HEADS, 1), F32),
                        pltpu.VMEM((SB_HEADS * pp, page), F32), pltpu.VMEM((SB_HEADS * pp, page), F32)],
    )
    out = pl.pallas_call(
        functools.partial(_sb_sample_kernel, pp=pp, n_steps=n_steps),
        grid_spec=grid_spec,
        out_shape=jax.ShapeDtypeStruct((b, SB_HEADS, SB_HEAD_DIM, 1), F32),
        compiler_params=_cparams(2),
        name="sb_sample",
    )(page_table, q.reshape(b, SB_HEADS, SB_HEAD_DIM, 1), bias.reshape(SB_HEADS, 1),
      *([k_pool] * pp), *([v_pool] * pp), jnp.asarray(tri, BF16))
    return out.reshape(b, SB_WIDTH)


def _rw_prep(p, prev, mu, w0, a0, w2a, k_k, k_a, r_k, gsum):
    u = p + mu * (prev - p)
    r = u[:, 0:RW_WIDTH]
    k = u[:, RW_WIDTH:2 * RW_WIDTH]
    v = u[:, 2 * RW_WIDTH:3 * RW_WIDTH]
    x2 = u[:, 3 * RW_WIDTH:]
    lane = lax.broadcasted_iota(jnp.int32, x2.shape, 1)
    x2 = jnp.where(lane < RW_LORA, jnp.tanh(x2), x2)
    d = _dot(x2.astype(BF16), w2a)
    w_log = -_softplus(-(w0 + d[:, :RW_WIDTH])) - 0.5
    lw = -jnp.exp(w_log)
    a = _sigmoid(a0 + d[:, RW_WIDTH:])
    kk = k * k_k
    kk = kk / jnp.maximum(jnp.sqrt(_dot_hilo(kk * kk, gsum)), 1e-12)
    k = k * (1.0 + (a - 1.0) * k_a)
    bonus = _dot_hilo(r * k * r_k, gsum) * v
    return r, lw, k, v, kk, kk * a, bonus


def _neumann_inverse(a):
    n = a.shape[0]
    eye = (lax.broadcasted_iota(jnp.int32, (n, n), 0) == lax.broadcasted_iota(jnp.int32, (n, n), 1)).astype(F32)
    x = eye - a
    p = a
    order = 2
    while order < RW_CHUNK:
        pb = p.astype(BF16)
        p = _dot(pb, pb)
        x = x + _dot(x.astype(BF16), p.astype(BF16))
        order *= 2
    return x


def _rwkv_prompt_kernel(p_ref, prev_ref, s0_ref, mu_ref, w0_ref, a0_ref, w2a_ref, kk_ref, ka_ref, rk_ref,
                        lng_ref, lnb_ref, gsum_ref, lincl_ref, ones_ref,
                        y_ref, sout_ref,
                        s_scr, last_scr, r_scr, lw_scr, k_scr, v_scr, kk_scr, b_scr, y_scr, *, tb):
    i = pl.program_id(0)

    @pl.when(i == 0)
    def _():
        s_scr[...] = s0_ref[...]
        last_scr[...] = prev_ref[...]

    p = p_ref[...]
    row0 = lax.broadcasted_iota(jnp.int32, p.shape, 0) == 0
    prev = jnp.where(row0, last_scr[...], pltpu.roll(p, 1, axis=0))
    last_scr[...] = p[tb - 1:tb, :]
    gsum = gsum_ref[...]
    r, lw, k, v, kk, b, bonus = _rw_prep(p, prev, mu_ref[...], w0_ref[...], a0_ref[...], w2a_ref[...],
                                         kk_ref[...], ka_ref[...], rk_ref[...], gsum)
    r_scr[...] = r
    lw_scr[...] = lw
    k_scr[...] = k
    v_scr[...] = v
    kk_scr[...] = kk
    b_scr[...] = b
    y_ref[...] = bonus

    n = RW_GROUP * RW_CHUNK
    row = lax.broadcasted_iota(jnp.int32, (n, n), 0)
    col = lax.broadcasted_iota(jnp.int32, (n, n), 1)
    same_head = (row // RW_CHUNK) == (col // RW_HEAD_DIM)
    strict = col < row
    incl = col <= row

    def chunk(ci, _):
        rows = pl.ds(pl.multiple_of(ci * RW_CHUNK, RW_CHUNK), RW_CHUNK)
        lwc = lw_scr[rows, :]
        lhi, llo = _split_bf16(lwc)
        lincl = lincl_ref[...]
        cum = _dot(lincl, lhi) + _dot(lincl, llo)
        cum_end = cum[RW_CHUNK - 1:RW_CHUNK, :]
        ones = ones_ref[...]
        g_end_col = jnp.exp(_dot_tn(lhi, ones) + _dot_tn(llo, ones))
        g_in = jnp.exp(cum)
        g_inv = jnp.exp(-cum)
        g_out = jnp.exp(cum_end - cum)
        rc = r_scr[rows, :] * g_in
        kkc = kk_scr[rows, :] * jnp.exp(cum - lwc)
        bc = b_scr[rows, :]
        kc = k_scr[rows, :]
        vc = v_scr[rows, :]
        bt = bc * g_inv
        kt = kc * g_inv
        bh = bc * g_out
        kh = kc * g_out
        for g in range(RW_NGROUP):
            sl = slice(g * n, (g + 1) * n)

            def bdiag(x):
                return jnp.where(same_head, jnp.concatenate([x[:, sl]] * RW_GROUP, axis=0), 0.0).astype(BF16)

            kk_b, r_b, bt_b, kt_b, bh_b, kh_b, v_b = (bdiag(x) for x in (kkc, rc, bt, kt, bh, kh, vc))
            a_b = jnp.where(strict, _dot_nt(kk_b, bt_b), 0.0)
            a_k = jnp.where(strict, _dot_nt(kk_b, kt_b), 0.0).astype(BF16)
            p_b = jnp.where(incl, _dot_nt(r_b, bt_b), 0.0).astype(BF16)
            p_k = jnp.where(incl, _dot_nt(r_b, kt_b), 0.0).astype(BF16)
            t_inv = _neumann_inverse(a_b).astype(BF16)
            s_old = s_scr[g]
            s_bf = s_old.astype(BF16)
            u = -_dot(t_inv, (_dot(kk_b, s_bf) + _dot(a_k, v_b)).astype(BF16))
            u_bf = u.astype(BF16)
            y = _dot(r_b, s_bf) + _dot(p_b, u_bf) + _dot(p_k, v_b)
            g_end = g_end_col[sl, :]
            s_scr[g] = (jnp.concatenate([g_end] * (n // 128), axis=1) * s_old
                        + _dot_tn(bh_b, u_bf) + _dot_tn(kh_b, v_b))
            yg = y[0:RW_CHUNK]
            for hh in range(1, RW_GROUP):
                yg = yg + y[hh * RW_CHUNK:(hh + 1) * RW_CHUNK]
            y_scr[rows, sl] = yg
        return 0

    lax.fori_loop(0, tb // RW_CHUNK, chunk, 0)

    y = y_scr[...]
    inv_n = 1.0 / RW_HEAD_DIM
    mean = _dot_hilo(y, gsum) * inv_n
    d = y - mean
    var = _dot_hilo(d * d, gsum) * inv_n
    y_ref[...] = d * lax.rsqrt(var + GN_EPS) * lng_ref[...] + lnb_ref[...] + y_ref[...]

    @pl.when(i == pl.num_programs(0) - 1)
    def _():
        sout_ref[...] = s_scr[...]


def _rw_consts():
    hd = np.arange(RW_WIDTH) // RW_HEAD_DIM
    gsum = (hd[:, None] == hd[None, :]).astype(np.float32)
    lincl = (np.arange(RW_CHUNK)[:, None] >= np.arange(RW_CHUNK)[None, :]).astype(np.float32)
    ones = np.ones((RW_CHUNK, 128), np.float32)
    return jnp.asarray(gsum, BF16), jnp.asarray(lincl, BF16), jnp.asarray(ones, BF16)


def _rw_params(lp):
    row = lambda a: a.reshape(1, -1)
    z = jnp.zeros((RW_LORA, RW_WIDTH), F32)
    w2a = jnp.concatenate([jnp.concatenate([lp["rw_w2"], z], axis=1),
                           jnp.concatenate([z, lp["rw_a2"]], axis=1)], axis=0).astype(BF16)
    return dict(mu=row(lp["shift_mu"]), w0=row(lp["rw_w0"]), a0=row(lp["rw_a0"]), w2a=w2a,
                k_k=row(lp["rw_k_k"]), k_a=row(lp["rw_k_a"]), r_k=row(lp["rw_r_k"]),
                ln_g=row(lp["rw_ln_g"]), ln_b=row(lp["rw_ln_b"]))


def _rwkv_prompt(proj, prev_row, s0_bd, rwp, tb):
    t = proj.shape[0]
    gsum, lincl, ones = _rw_consts()
    n = RW_GROUP * RW_CHUNK
    const = lambda shape: pl.BlockSpec(shape, lambda i: (0,) * len(shape))
    vec = const((1, RW_WIDTH))
    return pl.pallas_call(
        functools.partial(_rwkv_prompt_kernel, tb=tb),
        grid=(t // tb,),
        in_specs=[pl.BlockSpec((tb, RW_SHIFT_WIDTH), lambda i: (i, COL_RP // RW_SHIFT_WIDTH)),
                  const((1, RW_SHIFT_WIDTH)), const((RW_NGROUP, n, n)), const((1, RW_SHIFT_WIDTH)),
                  vec, vec, const((2 * RW_LORA, 2 * RW_WIDTH)), vec, vec, vec, vec, vec,
                  const((RW_WIDTH, RW_WIDTH)), const((RW_CHUNK, RW_CHUNK)), const((RW_CHUNK, 128))],
        out_specs=[pl.BlockSpec((tb, RW_WIDTH), lambda i: (i, 0)), const((RW_NGROUP, n, n))],
        out_shape=[jax.ShapeDtypeStruct((t, RW_WIDTH), F32), jax.ShapeDtypeStruct((RW_NGROUP, n, n), F32)],
        scratch_shapes=[pltpu.VMEM((RW_NGROUP, n, n), F32), pltpu.VMEM((1, RW_SHIFT_WIDTH), F32)]
        + [pltpu.VMEM((tb, RW_WIDTH), F32)] * 7,
        compiler_params=_cparams(1),
        name="rwkv_prompt",
    )(proj, prev_row, s0_bd, rwp["mu"], rwp["w0"], rwp["a0"], rwp["w2a"], rwp["k_k"], rwp["k_a"], rwp["r_k"],
      rwp["ln_g"], rwp["ln_b"], gsum, lincl, ones)


def _state_to_bd(s):
    st = jnp.swapaxes(s, -1, -2).reshape(RW_NGROUP, RW_GROUP, RW_HEAD_DIM, RW_HEAD_DIM)
    eye = jnp.eye(RW_GROUP, dtype=s.dtype)
    bd = st[:, :, :, None, :] * eye[None, :, None, :, None]
    n = RW_GROUP * RW_HEAD_DIM
    return bd.reshape(RW_NGROUP, n, n)


def _bd_to_state(bd):
    n = RW_GROUP * RW_HEAD_DIM
    b5 = bd.reshape(RW_NGROUP, RW_GROUP, RW_HEAD_DIM, RW_GROUP, RW_HEAD_DIM)
    idx = jnp.arange(RW_GROUP)
    blocks = b5[:, idx, :, idx, :]
    blocks = jnp.swapaxes(blocks, 0, 1).reshape(RW_HEADS, RW_HEAD_DIM, RW_HEAD_DIM)
    return jnp.swapaxes(blocks, -1, -2)


def _rwkv_sample_prep_kernel(p_ref, prev_ref, mu_ref, w0_ref, a0_ref, w2a_ref, kk_ref, ka_ref, rk_ref, gsum_ref,
                             r_ref, w_ref, k_ref, v_ref, kko_ref, b_ref, bonus_ref):
    r, lw, k, v, kk, b, bonus = _rw_prep(p_ref[...], prev_ref[...], mu_ref[...], w0_ref[...], a0_ref[...],
                                         w2a_ref[...], kk_ref[...], ka_ref[...], rk_ref[...], gsum_ref[...])
    r_ref[...] = r
    w_ref[...] = jnp.exp(lw)
    k_ref[...] = k
    v_ref[...] = v
    kko_ref[...] = kk
    b_ref[...] = b
    bonus_ref[...] = bonus


def _rwkv_sample_state_kernel(s_ref, r_ref, w_ref, k_ref, kk_ref, b_ref, v_ref, bonus_ref, lng_ref, lnb_ref,
                              y_ref, so_ref):
    s = s_ref[0]
    sa = jnp.sum(s * kk_ref[0], axis=-1, keepdims=True)
    s = s * w_ref[0] - sa * b_ref[0] + v_ref[0] * k_ref[0]
    so_ref[0] = s
    y = jnp.sum(s * r_ref[0], axis=-1, keepdims=True)
    mean = jnp.mean(y, axis=1, keepdims=True)
    d = y - mean
    var = jnp.mean(d * d, axis=1, keepdims=True)
    y_ref[0] = d * lax.rsqrt(var + GN_EPS) * lng_ref[...] + lnb_ref[...] + bonus_ref[0]


def _rwkv_sample(proj, prev_rows, state, rwp):
    b = proj.shape[0]
    gsum, _, _ = _rw_consts()
    full = lambda shape: pl.BlockSpec(shape, lambda i: (0,) * len(shape))
    vec = full((1, RW_WIDTH))
    outs = pl.pallas_call(
        _rwkv_sample_prep_kernel,
        grid=(1,),
        in_specs=[pl.BlockSpec((b, RW_SHIFT_WIDTH), lambda i: (0, COL_RP // RW_SHIFT_WIDTH)),
                  full((b, RW_SHIFT_WIDTH)), full((1, RW_SHIFT_WIDTH)), vec, vec,
                  full((2 * RW_LORA, 2 * RW_WIDTH)), vec, vec, vec, full((RW_WIDTH, RW_WIDTH))],
        out_specs=[full((b, RW_WIDTH))] * 7,
        out_shape=[jax.ShapeDtypeStruct((b, RW_WIDTH), F32)] * 7,
        compiler_params=_cparams(1),
        name="rwkv_sample_prep",
    )(proj, prev_rows, rwp["mu"], rwp["w0"], rwp["a0"], rwp["w2a"], rwp["k_k"], rwp["k_a"], rwp["r_k"], gsum)
    r, w, k, v, kk, bb, bonus = outs
    as_row = lambda a: a.reshape(b, RW_HEADS, 1, RW_HEAD_DIM)
    as_col = lambda a: a.reshape(-1, RW_HEADS, RW_HEAD_DIM, 1)
    row_spec = pl.BlockSpec((1, RW_HEADS, 1, RW_HEAD_DIM), lambda i: (i, 0, 0, 0))
    col_spec = pl.BlockSpec((1, RW_HEADS, RW_HEAD_DIM, 1), lambda i: (i, 0, 0, 0))
    par_spec = pl.BlockSpec((RW_HEADS, RW_HEAD_DIM, 1), lambda i: (0, 0, 0))
    s_spec = pl.BlockSpec((1, RW_HEADS, RW_HEAD_DIM, RW_HEAD_DIM), lambda i: (i, 0, 0, 0))
    y, s_new = pl.pallas_call(
        _rwkv_sample_state_kernel,
        grid=(b,),
        in_specs=[s_spec, row_spec, row_spec, row_spec, row_spec, row_spec, col_spec, col_spec, par_spec, par_spec],
        out_specs=[col_spec, s_spec],
        out_shape=[jax.ShapeDtypeStruct((b, RW_HEADS, RW_HEAD_DIM, 1), F32),
                   jax.ShapeDtypeStruct(state.shape, F32)],
        compiler_params=_cparams(1),
        name="rwkv_sample_state",
    )(state, as_row(r), as_row(w), as_row(k), as_row(kk), as_row(bb), as_col(v), as_col(bonus),
      rwp["ln_g"].reshape(RW_HEADS, RW_HEAD_DIM, 1), rwp["ln_b"].reshape(RW_HEADS, RW_HEAD_DIM, 1))
    return y.reshape(b, RW_WIDTH), s_new


def _mem_prompt_kernel(q_ref, mk_ref, mv_ref, o_ref):
    scale = MEM_HEAD_DIM ** -0.5
    for h in range(MEM_HEADS):
        sl = slice(h * MEM_HEAD_DIM, (h + 1) * MEM_HEAD_DIM)
        s = _dot_nt(q_ref[:, sl].astype(BF16), mk_ref[:, sl].astype(BF16)) * scale
        p = jnp.exp(s - jnp.max(s, axis=1, keepdims=True))
        l = jnp.sum(p, axis=1, keepdims=True)
        o_ref[:, sl] = _dot(p.astype(BF16), mv_ref[:, sl].astype(BF16)) / l


def _mem_prompt(proj, mk, mv, tm):
    t = proj.shape[0]
    m = mk.shape[0]
    return pl.pallas_call(
        _mem_prompt_kernel,
        grid=(t // tm,),
        in_specs=[pl.BlockSpec((tm, MEM_WIDTH), lambda i: (i, COL_MQ // MEM_WIDTH)),
                  pl.BlockSpec((m, MEM_WIDTH), lambda i: (0, 0)),
                  pl.BlockSpec((m, MEM_WIDTH), lambda i: (0, 0))],
        out_specs=pl.BlockSpec((tm, MEM_WIDTH), lambda i: (i, 0)),
        out_shape=jax.ShapeDtypeStruct((t, MEM_WIDTH), F32),
        compiler_params=_cparams(1),
        name="mem_prompt",
    )(proj, mk, mv)


def _mem_sample_kernel(q_ref, mk_ref, mv_ref, o_ref):
    scale = MEM_HEAD_DIM ** -0.5
    q = q_ref[0]
    rowh = lax.broadcasted_iota(jnp.int32, (8, MEM_WIDTH), 0)
    laneh = lax.broadcasted_iota(jnp.int32, (8, MEM_WIDTH), 1) // MEM_HEAD_DIM
    own = rowh == laneh
    qblk = jnp.where(own, q, 0.0).astype(BF16)
    s = _dot_nt(qblk, mk_ref[0].astype(BF16)) * scale
    p = jnp.exp(s - jnp.max(s, axis=1, keepdims=True))
    l = jnp.sum(p, axis=1, keepdims=True)
    o = _dot(p.astype(BF16), mv_ref[0].astype(BF16)) / l
    o_ref[0] = jnp.sum(jnp.where(own, o, 0.0), axis=0, keepdims=True)


def _mem_sample(q, mk, mv):
    b, m, _ = mk.shape
    q_spec = pl.BlockSpec((1, 1, MEM_WIDTH), lambda i: (i, 0, 0))
    kv_spec = pl.BlockSpec((1, m, MEM_WIDTH), lambda i: (i, 0, 0))
    out = pl.pallas_call(
        _mem_sample_kernel,
        grid=(b,),
        in_specs=[q_spec, kv_spec, kv_spec],
        out_specs=q_spec,
        out_shape=jax.ShapeDtypeStruct((b, 1, MEM_WIDTH), F32),
        compiler_params=_cparams(1),
        name="mem_sample",
    )(q.reshape(b, 1, MEM_WIDTH), mk, mv)
    return out.reshape(b, MEM_WIDTH)


def _merge_kernel(x_ref, gl_ref, osb_ref, sz_ref, orw_ref, rz_ref, omem_ref, mz_ref,
                  wsb_ref, wrw_ref, wmem_ref, wo_ref, fg_ref, o_ref, *, final):
    a_sb = (osb_ref[...] * _silu(sz_ref[...])).astype(BF16)
    a_rw = (orw_ref[...] * _silu(rz_ref[...])).astype(BF16)
    a_mem = (omem_ref[...] * _silu(mz_ref[...])).astype(BF16)
    merged = (_sigmoid(gl_ref[:, 0:D_MODEL]) * _dot(a_sb, wsb_ref[...])
              + _sigmoid(gl_ref[:, D_MODEL:2 * D_MODEL]) * _dot(a_rw, wrw_ref[...])
              + _sigmoid(gl_ref[:, 2 * D_MODEL:3 * D_MODEL]) * _dot(a_mem, wmem_ref[...]))
    y = x_ref[...] + _dot(merged.astype(BF16), wo_ref[...])
    if final:
        ms = jnp.mean(y * y, axis=-1, keepdims=True)
        y = y * lax.rsqrt(ms + RMS_EPS) * fg_ref[...]
    o_ref[...] = y


def _merge(x, proj, o_sb, o_rw, o_mem, w_sb, w_rw, w_mem, w_o, final_g, final, tm, name):
    t = x.shape[0]
    col = lambda c: pl.BlockSpec((tm, SB_WIDTH), lambda i: (i, c // SB_WIDTH))
    act = pl.BlockSpec((tm, SB_WIDTH), lambda i: (i, 0))
    wspec = pl.BlockSpec((SB_WIDTH, D_MODEL), lambda i: (0, 0))
    return pl.pallas_call(
        functools.partial(_merge_kernel, final=final),
        grid=(t // tm,),
        in_specs=[pl.BlockSpec((tm, D_MODEL), lambda i: (i, 0)),
                  pl.BlockSpec((tm, 3 * D_MODEL), lambda i: (i, 0)),
                  act, col(COL_SZ), act, col(COL_RZ), act, col(COL_MZ),
                  wspec, wspec, wspec, pl.BlockSpec((D_MODEL, D_MODEL), lambda i: (0, 0)),
                  pl.BlockSpec((1, D_MODEL), lambda i: (0, 0))],
        out_specs=pl.BlockSpec((tm, D_MODEL), lambda i: (i, 0)),
        out_shape=jax.ShapeDtypeStruct((t, D_MODEL), F32),
        compiler_params=_cparams(1),
        name=name,
    )(x, proj, o_sb, proj, o_rw, proj, o_mem, proj, w_sb, w_rw, w_mem, w_o, final_g.reshape(1, D_MODEL))


def _permute_w_in(w_in):
    sq, sk, sv, sz, rp, rz, mq, mz, gl = jnp.split(
        w_in, np.cumsum([SB_WIDTH] * 4 + [RW_SHIFT_WIDTH, RW_WIDTH, MEM_WIDTH, MEM_WIDTH]).tolist(), axis=1)
    return jnp.concatenate([gl, sq, sk, sv, sz, rz, mq, mz, rp], axis=1).astype(BF16)


def _heads_major(x, scale=None):
    t = x.shape[0]
    if scale is not None:
        x = x * scale
    return jnp.transpose(x.reshape(t, SB_HEADS, SB_HEAD_DIM), (1, 0, 2)).astype(BF16)


def kernel(x_prompt, x_sample, cache_sb_k, cache_sb_v, cache_mem_k, cache_mem_v, state_wkv, state_shift, page_table, mem_prompt, norm_g, w_in, sb_bias, shift_mu, rw_w0, rw_w2, rw_a0, rw_a2, rw_k_k, rw_k_a, rw_r_k, rw_ln_g, rw_ln_b, mem_norm_g, w_mem_kv, w_bo_sb, w_bo_rw, w_bo_mem, w_o, final_norm_g):
    depth = w_in.shape[0]
    bp, tp, _ = x_prompt.shape
    bs, ts, _ = x_sample.shape
    assert bp == 1 and ts == 1
    n_pool, page = cache_sb_k.shape[1:3]
    n_mem = mem_prompt.shape[1]
    sb_scale = SB_HEAD_DIM ** -0.5

    hp = x_prompt.reshape(tp, D_MODEL)
    hs = x_sample.reshape(bs, D_MODEL)
    mem = mem_prompt.reshape(n_mem, D_MODEL)
    k_pool_t = jnp.transpose(cache_sb_k, (0, 1, 3, 4, 2))
    v_pool_t = jnp.transpose(cache_sb_v, (0, 1, 3, 4, 2))
    outs = {k: [] for k in ("sbk_p", "sbv_p", "mk_p", "mv_p", "wkv_p", "shift_p", "sbk_s", "sbv_s", "wkv_s", "shift_s")}
    for l in range(depth):
        last = l == depth - 1
        lp = dict(shift_mu=shift_mu[l], rw_w0=rw_w0[l], rw_w2=rw_w2[l], rw_a0=rw_a0[l], rw_a2=rw_a2[l],
                  rw_k_k=rw_k_k[l], rw_k_a=rw_k_a[l], rw_r_k=rw_r_k[l], rw_ln_g=rw_ln_g[l], rw_ln_b=rw_ln_b[l])
        rwp = _rw_params(lp)
        w_perm = _permute_w_in(w_in[l])
        w_sb, w_rw, w_mem, w_out = (w.astype(BF16) for w in (w_bo_sb[l], w_bo_rw[l], w_bo_mem[l], w_o[l]))

        proj = _norm_proj(hp, norm_g[l], w_perm, 1024, 640, "proj_prompt")
        kv = _norm_proj(mem, mem_norm_g[l], w_mem_kv[l].astype(BF16), n_mem, MEM_WIDTH, "mem_kv")
        mk, mv = kv[:, :MEM_WIDTH], kv[:, MEM_WIDTH:]
        sk, sv = proj[:, COL_SK:COL_SK + SB_WIDTH], proj[:, COL_SV:COL_SV + SB_WIDTH]
        q_hm = _heads_major(proj[:, COL_SQ:COL_SQ + SB_WIDTH], sb_scale)
        kt_hm = jnp.transpose(sk.reshape(tp, SB_HEADS, SB_HEAD_DIM), (1, 2, 0)).astype(BF16)
        o_sb = _sb_prompt(q_hm, kt_hm, _heads_major(sv), sb_bias[l], 512, 1024)
        o_sb = jnp.transpose(o_sb, (1, 0, 2)).reshape(tp, SB_WIDTH)
        s0 = _state_to_bd(jnp.zeros((RW_HEADS, RW_HEAD_DIM, RW_HEAD_DIM), F32))
        o_rw, s_bd = _rwkv_prompt(proj, jnp.zeros((1, RW_SHIFT_WIDTH), F32), s0, rwp, 512)
        o_mem = _mem_prompt(proj, mk, mv, 512)
        hp = _merge(hp, proj, o_sb, o_rw, o_mem, w_sb, w_rw, w_mem, w_out, final_norm_g, last, 256, "merge_prompt")
        outs["sbk_p"].append(sk.reshape(bp, tp // page, page, SB_HEADS, SB_HEAD_DIM))
        outs["sbv_p"].append(sv.reshape(bp, tp // page, page, SB_HEADS, SB_HEAD_DIM))
        outs["mk_p"].append(mk.reshape(bp, n_mem, MEM_HEADS, MEM_HEAD_DIM))
        outs["mv_p"].append(mv.reshape(bp, n_mem, MEM_HEADS, MEM_HEAD_DIM))
        outs["wkv_p"].append(_bd_to_state(s_bd).reshape(bp, RW_HEADS, RW_HEAD_DIM, RW_HEAD_DIM))
        outs["shift_p"].append(proj[tp - 1:tp, COL_RP:])

        proj_s = _norm_proj(hs, norm_g[l], w_perm, bs, 640, "proj_sample")
        sk_s, sv_s = proj_s[:, COL_SK:COL_SK + SB_WIDTH], proj_s[:, COL_SV:COL_SV + SB_WIDTH]
        o_sb_s = _sb_sample(proj_s[:, COL_SQ:COL_SQ + SB_WIDTH] * sb_scale, sb_bias[l], k_pool_t, v_pool_t, l,
                            page_table, 16)
        o_rw_s, s_new = _rwkv_sample(proj_s, state_shift[l], state_wkv[l], rwp)
        o_mem_s = _mem_sample(proj_s[:, COL_MQ:COL_MQ + MEM_WIDTH],
                              cache_mem_k[l].reshape(bs, n_mem, MEM_WIDTH),
                              cache_mem_v[l].reshape(bs, n_mem, MEM_WIDTH))
        hs = _merge(hs, proj_s, o_sb_s, o_rw_s, o_mem_s, w_sb, w_rw, w_mem, w_out, final_norm_g, last, bs,
                    "merge_sample")
        outs["sbk_s"].append(sk_s.reshape(bs, ts, SB_HEADS, SB_HEAD_DIM))
        outs["sbv_s"].append(sv_s.reshape(bs, ts, SB_HEADS, SB_HEAD_DIM))
        outs["wkv_s"].append(s_new)
        outs["shift_s"].append(proj_s[:, COL_RP:])

    st = {k: jnp.stack(v) for k, v in outs.items()}
    return (hp.reshape(bp, tp, D_MODEL), hs.reshape(bs, ts, D_MODEL),
            st["sbk_p"], st["sbv_p"], st["mk_p"], st["mv_p"], st["wkv_p"], st["shift_p"],
            st["sbk_s"], st["sbv_s"], st["wkv_s"], st["shift_s"])
```

```python
import functools

import numpy as np
import jax
import jax.numpy as jnp
from jax import lax
from jax.experimental import pallas as pl
from jax.experimental.pallas import tpu as pltpu

F32 = jnp.float32
BF16 = jnp.bfloat16

D_MODEL = 1024
SB_HEADS = 8
SB_HEAD_DIM = 64
SB_WIDTH = SB_HEADS * SB_HEAD_DIM
RW_HEADS = 8
RW_HEAD_DIM = 64
RW_WIDTH = RW_HEADS * RW_HEAD_DIM
RW_LORA = 64
RW_SHIFT_WIDTH = 3 * RW_WIDTH + 2 * RW_LORA
MEM_HEADS = 4
MEM_HEAD_DIM = 128
MEM_WIDTH = MEM_HEADS * MEM_HEAD_DIM
GN_EPS = 64e-5
RMS_EPS = 1e-6
C_IN = 4 * SB_WIDTH + RW_SHIFT_WIDTH + RW_WIDTH + 2 * MEM_WIDTH + 3 * D_MODEL

COL_GATE = 0
COL_SQ = 3 * D_MODEL
COL_SK = COL_SQ + SB_WIDTH
COL_SV = COL_SK + SB_WIDTH
COL_SZ = COL_SV + SB_WIDTH
COL_RZ = COL_SZ + SB_WIDTH
COL_MQ = COL_RZ + RW_WIDTH
COL_MZ = COL_MQ + MEM_WIDTH
COL_RP = COL_MZ + MEM_WIDTH
assert COL_RP + RW_SHIFT_WIDTH == C_IN and COL_RP % RW_SHIFT_WIDTH == 0

VMEM_LIMIT_BYTES = 56 * 1024 * 1024
MXU_TILE = 256

RW_CHUNK = 64
RW_GROUP = MXU_TILE // RW_CHUNK
RW_NGROUP = RW_HEADS // RW_GROUP
RW_LOCKSTEP = 4
SB_SUB = MXU_TILE


def _cparams(n_grid):
    return pltpu.CompilerParams(dimension_semantics=("arbitrary",) * n_grid,
                                vmem_limit_bytes=VMEM_LIMIT_BYTES)


def _sigmoid(x):
    return 1.0 / (1.0 + jnp.exp(-x))


def _softplus(x):
    return jnp.maximum(x, 0.0) + jnp.log(1.0 + jnp.exp(-jnp.abs(x)))


def _silu(x):
    return x * _sigmoid(x)


def _split_bf16(x):
    hi = x.astype(BF16)
    lo = (x - hi.astype(F32)).astype(BF16)
    return hi, lo


def _dot(a, b):
    return jnp.dot(a, b, preferred_element_type=F32)


def _dot_nt(a, b):
    return lax.dot_general(a, b, (((1,), (1,)), ((), ())), preferred_element_type=F32)


def _dot_tn(a, b):
    return lax.dot_general(a, b, (((0,), (0,)), ((), ())), preferred_element_type=F32)


def _dot_hilo(x, w01):
    hi, lo = _split_bf16(x)
    return _dot(hi, w01) + _dot(lo, w01)


def _norm_proj_kernel(x_ref, g_ref, w_ref, o_ref, h_ref):
    @pl.when(pl.program_id(1) == 0)
    def _():
        x = x_ref[...]
        ms = jnp.mean(x * x, axis=-1, keepdims=True)
        h_ref[...] = (x * lax.rsqrt(ms + RMS_EPS) * g_ref[...]).astype(BF16)

    o_ref[...] = _dot(h_ref[...], w_ref[...])


def _norm_proj(x, g, w, tm, tn, name):
    t, d = x.shape
    n = w.shape[1]
    return pl.pallas_call(
        _norm_proj_kernel,
        grid=(t // tm, n // tn),
        in_specs=[pl.BlockSpec((tm, d), lambda i, j: (i, 0)),
                  pl.BlockSpec((1, d), lambda i, j: (0, 0)),
                  pl.BlockSpec((d, tn), lambda i, j: (0, j))],
        out_specs=pl.BlockSpec((tm, tn), lambda i, j: (i, j)),
        out_shape=jax.ShapeDtypeStruct((t, n), F32),
        scratch_shapes=[pltpu.VMEM((tm, d), BF16)],
        compiler_params=_cparams(2),
        name=name,
    )(x, g.reshape(1, d), w)


def _sb_prompt_kernel(qi_ref, kj_ref, bias_ref, q_ref, kt_ref, v_ref, tri_ref, o_ref, acc_ref, carry_ref,
                      *, bq, bk):
    s = pl.program_id(0)
    qi = qi_ref[s]
    kj = kj_ref[s]
    diag = kj == (qi * bq) // bk

    @pl.when(diag)
    def _():
        acc_ref[...] = jnp.zeros_like(acc_ref)
        carry_ref[...] = jnp.zeros_like(carry_ref)

    def run(masked):
        def head(h, _):
            q = q_ref[h]
            bias = bias_ref[h]
            tri = tri_ref[...]
            c = carry_ref[h]
            acc = acc_ref[h]
            for kb in reversed(range(bk // SB_SUB)):
                ks = slice(kb * SB_SUB, (kb + 1) * SB_SUB)
                z = _dot(q, kt_ref[h, :, ks]) + bias
                neg_abs = pltpu.bitcast(pltpu.bitcast(z, jnp.uint32) | jnp.uint32(0x80000000), F32)
                sp = jnp.maximum(z, 0.0) + jnp.log(1.0 + jnp.exp(neg_abs))
                if masked:
                    row = lax.broadcasted_iota(jnp.int32, (bq, SB_SUB), 0) + qi * bq
                    col = lax.broadcasted_iota(jnp.int32, (bq, SB_SUB), 1) + (kj * bk + kb * SB_SUB)
                    vis = col < row
                    sp = jnp.where(vis, sp, 0.0)
                within = _dot(sp.astype(BF16), tri)
                w = jnp.exp(z - sp - within - c)
                if masked:
                    w = jnp.where(vis, w, 0.0)
                acc = acc + _dot(w.astype(BF16), v_ref[h, ks, :])
                c = c + jnp.sum(sp, axis=1, keepdims=True)
            carry_ref[h] = c
            acc_ref[h] = acc
            return 0

        lax.fori_loop(0, SB_HEADS, head, 0)

    @pl.when(diag)
    def _():
        run(True)

    @pl.when(jnp.logical_not(diag))
    def _():
        run(False)

    @pl.when(kj == 0)
    def _():
        o_ref[...] = acc_ref[...]


def _sb_prompt(q, kt, v, bias, bq, bk):
    h, t, d = q.shape
    nq = t // bq
    assert bk % bq == 0 and t % bk == 0
    first = [(i * bq) // bk for i in range(nq)]
    qi = np.concatenate([np.full(first[i] + 1, i) for i in range(nq)]).astype(np.int32)
    kj = np.concatenate([np.arange(first[i], -1, -1) for i in range(nq)]).astype(np.int32)
    tri = (np.arange(SB_SUB)[:, None] > np.arange(SB_SUB)[None, :]).astype(np.float32)
    grid_spec = pltpu.PrefetchScalarGridSpec(
        num_scalar_prefetch=2,
        grid=(len(qi),),
        in_specs=[pl.BlockSpec(memory_space=pltpu.SMEM),
                  pl.BlockSpec((h, bq, d), lambda s, qi, kj: (0, qi[s], 0)),
                  pl.BlockSpec((h, d, bk), lambda s, qi, kj: (0, 0, kj[s])),
                  pl.BlockSpec((h, bk, d), lambda s, qi, kj: (0, kj[s], 0)),
                  pl.BlockSpec((SB_SUB, SB_SUB), lambda s, qi, kj: (0, 0))],
        out_specs=pl.BlockSpec((h, bq, d), lambda s, qi, kj: (0, qi[s], 0)),
        scratch_shapes=[pltpu.VMEM((h, bq, d), F32), pltpu.VMEM((h, bq, 1), F32)],
    )
    return pl.pallas_call(
        functools.partial(_sb_prompt_kernel, bq=bq, bk=bk),
        grid_spec=grid_spec,
        out_shape=jax.ShapeDtypeStruct((h, t, d), F32),
        compiler_params=_cparams(1),
        name="sb_prompt",
    )(jnp.asarray(qi), jnp.asarray(kj), bias, q, kt, v, jnp.asarray(tri, BF16))


def _sb_sample_kernel(pt_ref, q_ref, bias_ref, *refs, pp, n_steps):
    k_refs = refs[:pp]
    v_refs = refs[pp:2 * pp]
    tri_ref, o_ref, acc_ref, carry_ref, z_scr, w_scr = refs[2 * pp:]
    g = pl.program_id(1)

    @pl.when(g == 0)
    def _():
        acc_ref[...] = jnp.zeros_like(acc_ref)
        carry_ref[...] = jnp.zeros_like(carry_ref)

    q = q_ref[0]
    for i in range(pp):
        for h in range(SB_HEADS):
            z_scr[i * SB_HEADS + h:i * SB_HEADS + h + 1, :] = jnp.sum(k_refs[i][0, 0, h] * q[h], axis=0, keepdims=True)
    z = z_scr[...] + jnp.concatenate([bias_ref[...]] * pp, axis=0)
    sp = _softplus(z)
    within = _dot_hilo(sp, tri_ref[...])
    tot = jnp.sum(sp, axis=1, keepdims=True)
    c = carry_ref[...]
    cs = [None] * pp
    for i in reversed(range(pp)):
        cs[i] = c
        c = c + tot[i * SB_HEADS:(i + 1) * SB_HEADS]
    carry_ref[...] = c
    w_scr[...] = jnp.exp(z - sp - within - jnp.concatenate(cs, axis=0))
    for h in range(SB_HEADS):
        a = acc_ref[h]
        for i in range(pp):
            a = a + v_refs[i][0, 0, h] * w_scr[i * SB_HEADS + h:i * SB_HEADS + h + 1, :]
        acc_ref[h] = a

    @pl.when(g == n_steps - 1)
    def _():
        o_ref[0] = jnp.sum(acc_ref[...], axis=2, keepdims=True)


def _sb_sample(q, bias, k_pool, v_pool, layer, page_table, pp):
    b = q.shape[0]
    n_pages = page_table.shape[1]
    page = k_pool.shape[-1]
    n_steps = n_pages // pp
    tri = (np.arange(page)[:, None] > np.arange(page)[None, :]).astype(np.float32)

    def page_spec(i):
        return pl.BlockSpec((1, 1, SB_HEADS, SB_HEAD_DIM, page),
                            lambda bi, g, pt: (layer, pt[bi, (n_steps - 1 - g) * pp + i], 0, 0, 0))

    q_spec = pl.BlockSpec((1, SB_HEADS, SB_HEAD_DIM, 1), lambda bi, g, pt: (bi, 0, 0, 0))
    grid_spec = pltpu.PrefetchScalarGridSpec(
        num_scalar_prefetch=1,
        grid=(b, n_steps),
        in_specs=[q_spec, pl.BlockSpec((SB_HEADS, 1), lambda bi, g, pt: (0, 0))]
        + [page_spec(i) for i in range(pp)] + [page_spec(i) for i in range(pp)]
        + [pl.BlockSpec((page, page), lambda bi, g, pt: (0, 0))],
        out_specs=q_spec,
        scratch_shapes=[pltpu.VMEM((SB_HEADS, SB_HEAD_DIM, page), F32), pltpu.VMEM((SB_HEADS, 1), F32),
                        pltpu.VMEM((SB_HEADS * pp, page), F32), pltpu.VMEM((SB_HEADS * pp, page), F32)],
    )
    out = pl.pallas_call(
        functools.partial(_sb_sample_kernel, pp=pp, n_steps=n_steps),
        grid_spec=grid_spec,
        out_shape=jax.ShapeDtypeStruct((b, SB_HEADS, SB_HEAD_DIM, 1), F32),
        compiler_params=_cparams(2),
        name="sb_sample",
    )(page_table, q.reshape(b, SB_HEADS, SB_HEAD_DIM, 1), bias.reshape(SB_HEADS, 1),
      *([k_pool] * pp), *([v_pool] * pp), jnp.asarray(tri, BF16))
    return out.reshape(b, SB_WIDTH)


def _rw_prep(p, prev, mu, w0, a0, w2a, k_k, k_a, r_k, gsum):
    u = p + mu * (prev - p)
    r = u[:, 0:RW_WIDTH]
    k = u[:, RW_WIDTH:2 * RW_WIDTH]
    v = u[:, 2 * RW_WIDTH:3 * RW_WIDTH]
    x2 = u[:, 3 * RW_WIDTH:]
    lane = lax.broadcasted_iota(jnp.int32, x2.shape, 1)
    x2 = jnp.where(lane < RW_LORA, jnp.tanh(x2), x2)
    d = _dot(x2.astype(BF16), w2a)
    w_log = -_softplus(-(w0 + d[:, :RW_WIDTH])) - 0.5
    lw = -jnp.exp(w_log)
    a = _sigmoid(a0 + d[:, RW_WIDTH:])
    kk = k * k_k
    kk = kk / jnp.maximum(jnp.sqrt(_dot_hilo(kk * kk, gsum)), 1e-12)
    k = k * (1.0 + (a - 1.0) * k_a)
    bonus = _dot_hilo(r * k * r_k, gsum) * v
    return r, lw, k, v, kk, kk * a, bonus


def _neumann_inverse(a_list, eye, bdiag):
    x = [eye - a for a in a_list]
    p = [_dot(a.astype(BF16), bdiag(a)) for a in a_list]
    order = 2
    while 2 * order < RW_CHUNK:
        pb = [bdiag(pi) for pi in p]
        x = [xi + _dot(xi.astype(BF16), pbi) for xi, pbi in zip(x, pb)]
        p = [_dot(pi.astype(BF16), pbi) for pi, pbi in zip(p, pb)]
        order *= 2
    return [xi + _dot(xi.astype(BF16), bdiag(pi)) for xi, pi in zip(x, p)]


def _rwkv_prompt_kernel(p_ref, prev_ref, s0_ref, mu_ref, w0_ref, a0_ref, w2a_ref, kk_ref, ka_ref, rk_ref,
                        lng_ref, lnb_ref, gsum_ref, lincl_ref, ones_ref,
                        y_ref, sout_ref,
                        s_scr, last_scr, r_scr, lw_scr, k_scr, v_scr, kk_scr, b_scr, y_scr, *, tb):
    i = pl.program_id(0)

    @pl.when(i == 0)
    def _():
        s_scr[...] = s0_ref[...]
        last_scr[...] = prev_ref[...]

    p = p_ref[...]
    row0 = lax.broadcasted_iota(jnp.int32, p.shape, 0) == 0
    prev = jnp.where(row0, last_scr[...], pltpu.roll(p, 1, axis=0))
    last_scr[...] = p[tb - 1:tb, :]
    gsum = gsum_ref[...]
    r, lw, k, v, kk, b, bonus = _rw_prep(p, prev, mu_ref[...], w0_ref[...], a0_ref[...], w2a_ref[...],
                                         kk_ref[...], ka_ref[...], rk_ref[...], gsum)
    r_scr[...] = r
    lw_scr[...] = lw
    k_scr[...] = k
    v_scr[...] = v
    kk_scr[...] = kk
    b_scr[...] = b
    y_ref[...] = bonus

    n = RW_GROUP * RW_CHUNK
    row = lax.broadcasted_iota(jnp.int32, (n, n), 0)
    col = lax.broadcasted_iota(jnp.int32, (n, n), 1)
    same_head = (row // RW_CHUNK) == (col // RW_HEAD_DIM)
    trow = lax.broadcasted_iota(jnp.int32, (RW_CHUNK, n), 0)
    tcol = lax.broadcasted_iota(jnp.int32, (RW_CHUNK, n), 1) % RW_CHUNK
    strict = tcol < trow
    incl = tcol <= trow
    eye = (tcol == trow).astype(F32)

    def bdiag(x):
        return jnp.where(same_head, jnp.concatenate([x] * RW_GROUP, axis=0), 0.0).astype(BF16)

    groups = range(RW_NGROUP)
    sls = [slice(g * n, (g + 1) * n) for g in groups]

    def decayed(ci):
        rows = pl.ds(pl.multiple_of(ci * RW_CHUNK, RW_CHUNK), RW_CHUNK)
        lwc = lw_scr[rows, :]
        lhi, llo = _split_bf16(lwc)
        lincl = lincl_ref[...]
        cum = _dot(lincl, lhi) + _dot(lincl, llo)
        cum_end = cum[RW_CHUNK - 1:RW_CHUNK, :]
        ones = ones_ref[...]
        g_end_col = jnp.exp(_dot_tn(lhi, ones) + _dot_tn(llo, ones))
        g_inv = jnp.exp(-cum)
        g_out = jnp.exp(cum_end - cum)
        rc = r_scr[rows, :] * jnp.exp(cum)
        kkc = kk_scr[rows, :] * jnp.exp(cum - lwc)
        bc = b_scr[rows, :]
        kc = k_scr[rows, :]
        return dict(rows=rows, g_end_col=g_end_col, rc=rc, kkc=kkc, vc=v_scr[rows, :],
                    bt=bc * g_inv, kt=kc * g_inv, bh=bc * g_out, kh=kc * g_out)

    def chunks(ci, _):
        cs = [decayed(ci * RW_LOCKSTEP + j) for j in range(RW_LOCKSTEP)]
        probs = [(c, sl) for c in cs for sl in sls]
        kr = [jnp.concatenate([c["kkc"][:, sl], c["rc"][:, sl]], axis=0).astype(BF16) for c, sl in probs]
        m_b = [_dot_nt(kr[i], bdiag(c["bt"][:, sl])) for i, (c, sl) in enumerate(probs)]
        m_k = [_dot_nt(kr[i], bdiag(c["kt"][:, sl])) for i, (c, sl) in enumerate(probs)]
        t_inv = _neumann_inverse([jnp.where(strict, m[:RW_CHUNK], 0.0) for m in m_b], eye, bdiag)
        ap_k = [jnp.concatenate([jnp.where(strict, m[:RW_CHUNK], 0.0), jnp.where(incl, m[RW_CHUNK:], 0.0)],
                                axis=0).astype(BF16) for m in m_k]
        from_v = [_dot(ap_k[i], bdiag(c["vc"][:, sl])) for i, (c, sl) in enumerate(probs)]
        p_b = [jnp.where(incl, m[RW_CHUNK:], 0.0).astype(BF16) for m in m_b]
        for j, c in enumerate(cs):
            ids = [j * RW_NGROUP + g for g in groups]
            s_old = [s_scr[g] for g in groups]
            from_s = [_dot(kr[i], s_old[g].astype(BF16)) for g, i in zip(groups, ids)]
            u = [-_dot(t_inv[i].astype(BF16), bdiag(from_s[g][:RW_CHUNK] + from_v[i][:RW_CHUNK]))
                 for g, i in zip(groups, ids)]
            for g, i in zip(groups, ids):
                c_y = from_s[g][RW_CHUNK:] + _dot(p_b[i], bdiag(u[g])) + from_v[i][RW_CHUNK:]
                y_scr[c["rows"], sls[g]] = c_y
            upd = [_dot_tn(jnp.concatenate([c["bh"][:, sls[g]], c["kh"][:, sls[g]]], axis=0).astype(BF16),
                           jnp.concatenate([u[g], c["vc"][:, sls[g]]], axis=0).astype(BF16)) for g in groups]
            for g in groups:
                g_end = jnp.concatenate([c["g_end_col"][sls[g], :]] * (n // 128), axis=1)
                s_scr[g] = g_end * s_old[g] + jnp.where(same_head, upd[g], 0.0)
        return 0

    lax.fori_loop(0, tb // (RW_CHUNK * RW_LOCKSTEP), chunks, 0)

    y = y_scr[...]
    inv_n = 1.0 / RW_HEAD_DIM
    mean = _dot_hilo(y, gsum) * inv_n
    d = y - mean
    var = _dot_hilo(d * d, gsum) * inv_n
    y_ref[...] = d * lax.rsqrt(var + GN_EPS) * lng_ref[...] + lnb_ref[...] + y_ref[...]

    @pl.when(i == pl.num_programs(0) - 1)
    def _():
        sout_ref[...] = s_scr[...]


def _rw_consts():
    hd = np.arange(RW_WIDTH) // RW_HEAD_DIM
    gsum = (hd[:, None] == hd[None, :]).astype(np.float32)
    lincl = (np.arange(RW_CHUNK)[:, None] >= np.arange(RW_CHUNK)[None, :]).astype(np.float32)
    ones = np.ones((RW_CHUNK, 128), np.float32)
    return jnp.asarray(gsum, BF16), jnp.asarray(lincl, BF16), jnp.asarray(ones, BF16)


def _rw_params(lp):
    row = lambda a: a.reshape(1, -1)
    z = jnp.zeros((RW_LORA, RW_WIDTH), F32)
    w2a = jnp.concatenate([jnp.concatenate([lp["rw_w2"], z], axis=1),
                           jnp.concatenate([z, lp["rw_a2"]], axis=1)], axis=0).astype(BF16)
    return dict(mu=row(lp["shift_mu"]), w0=row(lp["rw_w0"]), a0=row(lp["rw_a0"]), w2a=w2a,
                k_k=row(lp["rw_k_k"]), k_a=row(lp["rw_k_a"]), r_k=row(lp["rw_r_k"]),
                ln_g=row(lp["rw_ln_g"]), ln_b=row(lp["rw_ln_b"]))


def _rwkv_prompt(proj, prev_row, s0_bd, rwp, tb):
    t = proj.shape[0]
    gsum, lincl, ones = _rw_consts()
    n = RW_GROUP * RW_CHUNK
    const = lambda shape: pl.BlockSpec(shape, lambda i: (0,) * len(shape))
    vec = const((1, RW_WIDTH))
    return pl.pallas_call(
        functools.partial(_rwkv_prompt_kernel, tb=tb),
        grid=(t // tb,),
        in_specs=[pl.BlockSpec((tb, RW_SHIFT_WIDTH), lambda i: (i, COL_RP // RW_SHIFT_WIDTH)),
                  const((1, RW_SHIFT_WIDTH)), const((RW_NGROUP, n, n)), const((1, RW_SHIFT_WIDTH)),
                  vec, vec, const((2 * RW_LORA, 2 * RW_WIDTH)), vec, vec, vec, vec, vec,
                  const((RW_WIDTH, RW_WIDTH)), const((RW_CHUNK, RW_CHUNK)), const((RW_CHUNK, 128))],
        out_specs=[pl.BlockSpec((tb, RW_WIDTH), lambda i: (i, 0)), const((RW_NGROUP, n, n))],
        out_shape=[jax.ShapeDtypeStruct((t, RW_WIDTH), F32), jax.ShapeDtypeStruct((RW_NGROUP, n, n), F32)],
        scratch_shapes=[pltpu.VMEM((RW_NGROUP, n, n), F32), pltpu.VMEM((1, RW_SHIFT_WIDTH), F32)]
        + [pltpu.VMEM((tb, RW_WIDTH), F32)] * 7,
        compiler_params=_cparams(1),
        name="rwkv_prompt",
    )(proj, prev_row, s0_bd, rwp["mu"], rwp["w0"], rwp["a0"], rwp["w2a"], rwp["k_k"], rwp["k_a"], rwp["r_k"],
      rwp["ln_g"], rwp["ln_b"], gsum, lincl, ones)


def _state_to_bd(s):
    st = jnp.swapaxes(s, -1, -2).reshape(RW_NGROUP, RW_GROUP, RW_HEAD_DIM, RW_HEAD_DIM)
    eye = jnp.eye(RW_GROUP, dtype=s.dtype)
    bd = st[:, :, :, None, :] * eye[None, :, None, :, None]
    n = RW_GROUP * RW_HEAD_DIM
    return bd.reshape(RW_NGROUP, n, n)


def _bd_to_state(bd):
    n = RW_GROUP * RW_HEAD_DIM
    b5 = bd.reshape(RW_NGROUP, RW_GROUP, RW_HEAD_DIM, RW_GROUP, RW_HEAD_DIM)
    idx = jnp.arange(RW_GROUP)
    blocks = b5[:, idx, :, idx, :]
    blocks = jnp.swapaxes(blocks, 0, 1).reshape(RW_HEADS, RW_HEAD_DIM, RW_HEAD_DIM)
    return jnp.swapaxes(blocks, -1, -2)


def _rwkv_sample_prep_kernel(p_ref, prev_ref, mu_ref, w0_ref, a0_ref, w2a_ref, kk_ref, ka_ref, rk_ref, gsum_ref,
                             r_ref, w_ref, k_ref, v_ref, kko_ref, b_ref, bonus_ref):
    r, lw, k, v, kk, b, bonus = _rw_prep(p_ref[...], prev_ref[...], mu_ref[...], w0_ref[...], a0_ref[...],
                                         w2a_ref[...], kk_ref[...], ka_ref[...], rk_ref[...], gsum_ref[...])
    r_ref[...] = r
    w_ref[...] = jnp.exp(lw)
    k_ref[...] = k
    v_ref[...] = v
    kko_ref[...] = kk
    b_ref[...] = b
    bonus_ref[...] = bonus


def _rwkv_sample_state_kernel(s_ref, r_ref, w_ref, k_ref, kk_ref, b_ref, v_ref, bonus_ref, lng_ref, lnb_ref,
                              y_ref, so_ref):
    s = s_ref[0]
    sa = jnp.sum(s * kk_ref[0], axis=-1, keepdims=True)
    s = s * w_ref[0] - sa * b_ref[0] + v_ref[0] * k_ref[0]
    so_ref[0] = s
    y = jnp.sum(s * r_ref[0], axis=-1, keepdims=True)
    mean = jnp.mean(y, axis=1, keepdims=True)
    d = y - mean
    var = jnp.mean(d * d, axis=1, keepdims=True)
    y_ref[0] = d * lax.rsqrt(var + GN_EPS) * lng_ref[...] + lnb_ref[...] + bonus_ref[0]


def _rwkv_sample(proj, prev_rows, state, rwp):
    b = proj.shape[0]
    gsum, _, _ = _rw_consts()
    full = lambda shape: pl.BlockSpec(shape, lambda i: (0,) * len(shape))
    vec = full((1, RW_WIDTH))
    outs = pl.pallas_call(
        _rwkv_sample_prep_kernel,
        grid=(1,),
        in_specs=[pl.BlockSpec((b, RW_SHIFT_WIDTH), lambda i: (0, COL_RP // RW_SHIFT_WIDTH)),
                  full((b, RW_SHIFT_WIDTH)), full((1, RW_SHIFT_WIDTH)), vec, vec,
                  full((2 * RW_LORA, 2 * RW_WIDTH)), vec, vec, vec, full((RW_WIDTH, RW_WIDTH))],
        out_specs=[full((b, RW_WIDTH))] * 7,
        out_shape=[jax.ShapeDtypeStruct((b, RW_WIDTH), F32)] * 7,
        compiler_params=_cparams(1),
        name="rwkv_sample_prep",
    )(proj, prev_rows, rwp["mu"], rwp["w0"], rwp["a0"], rwp["w2a"], rwp["k_k"], rwp["k_a"], rwp["r_k"], gsum)
    r, w, k, v, kk, bb, bonus = outs
    as_row = lambda a: a.reshape(b, RW_HEADS, 1, RW_HEAD_DIM)
    as_col = lambda a: a.reshape(-1, RW_HEADS, RW_HEAD_DIM, 1)
    row_spec = pl.BlockSpec((1, RW_HEADS, 1, RW_HEAD_DIM), lambda i: (i, 0, 0, 0))
    col_spec = pl.BlockSpec((1, RW_HEADS, RW_HEAD_DIM, 1), lambda i: (i, 0, 0, 0))
    par_spec = pl.BlockSpec((RW_HEADS, RW_HEAD_DIM, 1), lambda i: (0, 0, 0))
    s_spec = pl.BlockSpec((1, RW_HEADS, RW_HEAD_DIM, RW_HEAD_DIM), lambda i: (i, 0, 0, 0))
    y, s_new = pl.pallas_call(
        _rwkv_sample_state_kernel,
        grid=(b,),
        in_specs=[s_spec, row_spec, row_spec, row_spec, row_spec, row_spec, col_spec, col_spec, par_spec, par_spec],
        out_specs=[col_spec, s_spec],
        out_shape=[jax.ShapeDtypeStruct((b, RW_HEADS, RW_HEAD_DIM, 1), F32),
                   jax.ShapeDtypeStruct(state.shape, F32)],
        compiler_params=_cparams(1),
        name="rwkv_sample_state",
    )(state, as_row(r), as_row(w), as_row(k), as_row(kk), as_row(bb), as_col(v), as_col(bonus),
      rwp["ln_g"].reshape(RW_HEADS, RW_HEAD_DIM, 1), rwp["ln_b"].reshape(RW_HEADS, RW_HEAD_DIM, 1))
    return y.reshape(b, RW_WIDTH), s_new


def _mem_prompt_kernel(q_ref, mk_ref, mv_ref, o_ref):
    scale = MEM_HEAD_DIM ** -0.5
    for h in range(MEM_HEADS):
        sl = slice(h * MEM_HEAD_DIM, (h + 1) * MEM_HEAD_DIM)
        s = _dot_nt(q_ref[:, sl].astype(BF16), mk_ref[:, sl].astype(BF16)) * scale
        p = jnp.exp(s - jnp.max(s, axis=1, keepdims=True))
        l = jnp.sum(p, axis=1, keepdims=True)
        o_ref[:, sl] = _dot(p.astype(BF16), mv_ref[:, sl].astype(BF16)) / l


def _mem_prompt(proj, mk, mv, tm):
    t = proj.shape[0]
    m = mk.shape[0]
    return pl.pallas_call(
        _mem_prompt_kernel,
        grid=(t // tm,),
        in_specs=[pl.BlockSpec((tm, MEM_WIDTH), lambda i: (i, COL_MQ // MEM_WIDTH)),
                  pl.BlockSpec((m, MEM_WIDTH), lambda i: (0, 0)),
                  pl.BlockSpec((m, MEM_WIDTH), lambda i: (0, 0))],
        out_specs=pl.BlockSpec((tm, MEM_WIDTH), lambda i: (i, 0)),
        out_shape=jax.ShapeDtypeStruct((t, MEM_WIDTH), F32),
        compiler_params=_cparams(1),
        name="mem_prompt",
    )(proj, mk, mv)


def _mem_sample_kernel(q_ref, mk_ref, mv_ref, o_ref):
    scale = MEM_HEAD_DIM ** -0.5
    q = q_ref[0]
    rowh = lax.broadcasted_iota(jnp.int32, (8, MEM_WIDTH), 0)
    laneh = lax.broadcasted_iota(jnp.int32, (8, MEM_WIDTH), 1) // MEM_HEAD_DIM
    own = rowh == laneh
    qblk = jnp.where(own, q, 0.0).astype(BF16)
    s = _dot_nt(qblk, mk_ref[0].astype(BF16)) * scale
    p = jnp.exp(s - jnp.max(s, axis=1, keepdims=True))
    l = jnp.sum(p, axis=1, keepdims=True)
    o = _dot(p.astype(BF16), mv_ref[0].astype(BF16)) / l
    o_ref[0] = jnp.sum(jnp.where(own, o, 0.0), axis=0, keepdims=True)


def _mem_sample(q, mk, mv):
    b, m, _ = mk.shape
    q_spec = pl.BlockSpec((1, 1, MEM_WIDTH), lambda i: (i, 0, 0))
    kv_spec = pl.BlockSpec((1, m, MEM_WIDTH), lambda i: (i, 0, 0))
    out = pl.pallas_call(
        _mem_sample_kernel,
        grid=(b,),
        in_specs=[q_spec, kv_spec, kv_spec],
        out_specs=q_spec,
        out_shape=jax.ShapeDtypeStruct((b, 1, MEM_WIDTH), F32),
        compiler_params=_cparams(1),
        name="mem_sample",
    )(q.reshape(b, 1, MEM_WIDTH), mk, mv)
    return out.reshape(b, MEM_WIDTH)


def _merge_kernel(x_ref, gl_ref, osb_ref, sz_ref, orw_ref, rz_ref, omem_ref, mz_ref,
                  wsb_ref, wrw_ref, wmem_ref, wo_ref, fg_ref, o_ref, *, final):
    a_sb = (osb_ref[...] * _silu(sz_ref[...])).astype(BF16)
    a_rw = (orw_ref[...] * _silu(rz_ref[...])).astype(BF16)
    a_mem = (omem_ref[...] * _silu(mz_ref[...])).astype(BF16)
    merged = (_sigmoid(gl_ref[:, 0:D_MODEL]) * _dot(a_sb, wsb_ref[...])
              + _sigmoid(gl_ref[:, D_MODEL:2 * D_MODEL]) * _dot(a_rw, wrw_ref[...])
              + _sigmoid(gl_ref[:, 2 * D_MODEL:3 * D_MODEL]) * _dot(a_mem, wmem_ref[...]))
    y = x_ref[...] + _dot(merged.astype(BF16), wo_ref[...])
    if final:
        ms = jnp.mean(y * y, axis=-1, keepdims=True)
        y = y * lax.rsqrt(ms + RMS_EPS) * fg_ref[...]
    o_ref[...] = y


def _merge(x, proj, o_sb, o_rw, o_mem, w_sb, w_rw, w_mem, w_o, final_g, final, tm, name):
    t = x.shape[0]
    col = lambda c: pl.BlockSpec((tm, SB_WIDTH), lambda i: (i, c // SB_WIDTH))
    act = pl.BlockSpec((tm, SB_WIDTH), lambda i: (i, 0))
    wspec = pl.BlockSpec((SB_WIDTH, D_MODEL), lambda i: (0, 0))
    return pl.pallas_call(
        functools.partial(_merge_kernel, final=final),
        grid=(t // tm,),
        in_specs=[pl.BlockSpec((tm, D_MODEL), lambda i: (i, 0)),
                  pl.BlockSpec((tm, 3 * D_MODEL), lambda i: (i, 0)),
                  act, col(COL_SZ), act, col(COL_RZ), act, col(COL_MZ),
                  wspec, wspec, wspec, pl.BlockSpec((D_MODEL, D_MODEL), lambda i: (0, 0)),
                  pl.BlockSpec((1, D_MODEL), lambda i: (0, 0))],
        out_specs=pl.BlockSpec((tm, D_MODEL), lambda i: (i, 0)),
        out_shape=jax.ShapeDtypeStruct((t, D_MODEL), F32),
        compiler_params=_cparams(1),
        name=name,
    )(x, proj, o_sb, proj, o_rw, proj, o_mem, proj, w_sb, w_rw, w_mem, w_o, final_g.reshape(1, D_MODEL))


def _permute_w_in(w_in):
    sq, sk, sv, sz, rp, rz, mq, mz, gl = jnp.split(
        w_in, np.cumsum([SB_WIDTH] * 4 + [RW_SHIFT_WIDTH, RW_WIDTH, MEM_WIDTH, MEM_WIDTH]).tolist(), axis=1)
    return jnp.concatenate([gl, sq, sk, sv, sz, rz, mq, mz, rp], axis=1).astype(BF16)


def _heads_major(x, scale=None):
    t = x.shape[0]
    if scale is not None:
        x = x * scale
    return jnp.transpose(x.reshape(t, SB_HEADS, SB_HEAD_DIM), (1, 0, 2)).astype(BF16)


def kernel(x_prompt, x_sample, cache_sb_k, cache_sb_v, cache_mem_k, cache_mem_v, state_wkv, state_shift, page_table, mem_prompt, norm_g, w_in, sb_bias, shift_mu, rw_w0, rw_w2, rw_a0, rw_a2, rw_k_k, rw_k_a, rw_r_k, rw_ln_g, rw_ln_b, mem_norm_g, w_mem_kv, w_bo_sb, w_bo_rw, w_bo_mem, w_o, final_norm_g):
    depth = w_in.shape[0]
    bp, tp, _ = x_prompt.shape
    bs, ts, _ = x_sample.shape
    assert bp == 1 and ts == 1
    n_pool, page = cache_sb_k.shape[1:3]
    n_mem = mem_prompt.shape[1]
    sb_scale = SB_HEAD_DIM ** -0.5

    hp = x_prompt.reshape(tp, D_MODEL)
    hs = x_sample.reshape(bs, D_MODEL)
    mem = mem_prompt.reshape(n_mem, D_MODEL)
    k_pool_t = jnp.transpose(cache_sb_k, (0, 1, 3, 4, 2))
    v_pool_t = jnp.transpose(cache_sb_v, (0, 1, 3, 4, 2))
    outs = {k: [] for k in ("sbk_p", "sbv_p", "mk_p", "mv_p", "wkv_p", "shift_p", "sbk_s", "sbv_s", "wkv_s", "shift_s")}
    for l in range(depth):
        last = l == depth - 1
        lp = dict(shift_mu=shift_mu[l], rw_w0=rw_w0[l], rw_w2=rw_w2[l], rw_a0=rw_a0[l], rw_a2=rw_a2[l],
                  rw_k_k=rw_k_k[l], rw_k_a=rw_k_a[l], rw_r_k=rw_r_k[l], rw_ln_g=rw_ln_g[l], rw_ln_b=rw_ln_b[l])
        rwp = _rw_params(lp)
        w_perm = _permute_w_in(w_in[l])
        w_sb, w_rw, w_mem, w_out = (w.astype(BF16) for w in (w_bo_sb[l], w_bo_rw[l], w_bo_mem[l], w_o[l]))

        proj = _norm_proj(hp, norm_g[l], w_perm, 1024, RW_SHIFT_WIDTH, "proj_prompt")
        kv = _norm_proj(mem, mem_norm_g[l], w_mem_kv[l].astype(BF16), n_mem, MEM_WIDTH, "mem_kv")
        mk, mv = kv[:, :MEM_WIDTH], kv[:, MEM_WIDTH:]
        sk, sv = proj[:, COL_SK:COL_SK + SB_WIDTH], proj[:, COL_SV:COL_SV + SB_WIDTH]
        q_hm = _heads_major(proj[:, COL_SQ:COL_SQ + SB_WIDTH], sb_scale)
        kt_hm = jnp.transpose(sk.reshape(tp, SB_HEADS, SB_HEAD_DIM), (1, 2, 0)).astype(BF16)
        o_sb = _sb_prompt(q_hm, kt_hm, _heads_major(sv), sb_bias[l], 512, 1024)
        o_sb = jnp.transpose(o_sb, (1, 0, 2)).reshape(tp, SB_WIDTH)
        s0 = _state_to_bd(jnp.zeros((RW_HEADS, RW_HEAD_DIM, RW_HEAD_DIM), F32))
        o_rw, s_bd = _rwkv_prompt(proj, jnp.zeros((1, RW_SHIFT_WIDTH), F32), s0, rwp, 512)
        o_mem = _mem_prompt(proj, mk, mv, 512)
        hp = _merge(hp, proj, o_sb, o_rw, o_mem, w_sb, w_rw, w_mem, w_out, final_norm_g, last, 256, "merge_prompt")
        outs["sbk_p"].append(sk.reshape(bp, tp // page, page, SB_HEADS, SB_HEAD_DIM))
        outs["sbv_p"].append(sv.reshape(bp, tp // page, page, SB_HEADS, SB_HEAD_DIM))
        outs["mk_p"].append(mk.reshape(bp, n_mem, MEM_HEADS, MEM_HEAD_DIM))
        outs["mv_p"].append(mv.reshape(bp, n_mem, MEM_HEADS, MEM_HEAD_DIM))
        outs["wkv_p"].append(_bd_to_state(s_bd).reshape(bp, RW_HEADS, RW_HEAD_DIM, RW_HEAD_DIM))
        outs["shift_p"].append(proj[tp - 1:tp, COL_RP:])

        proj_s = _norm_proj(hs, norm_g[l], w_perm, bs, RW_SHIFT_WIDTH, "proj_sample")
        sk_s, sv_s = proj_s[:, COL_SK:COL_SK + SB_WIDTH], proj_s[:, COL_SV:COL_SV + SB_WIDTH]
        o_sb_s = _sb_sample(proj_s[:, COL_SQ:COL_SQ + SB_WIDTH] * sb_scale, sb_bias[l], k_pool_t, v_pool_t, l,
                            page_table, 16)
        o_rw_s, s_new = _rwkv_sample(proj_s, state_shift[l], state_wkv[l], rwp)
        o_mem_s = _mem_sample(proj_s[:, COL_MQ:COL_MQ + MEM_WIDTH],
                              cache_mem_k[l].reshape(bs, n_mem, MEM_WIDTH),
                              cache_mem_v[l].reshape(bs, n_mem, MEM_WIDTH))
        hs = _merge(hs, proj_s, o_sb_s, o_rw_s, o_mem_s, w_sb, w_rw, w_mem, w_out, final_norm_g, last, bs,
                    "merge_sample")
        outs["sbk_s"].append(sk_s.reshape(bs, ts, SB_HEADS, SB_HEAD_DIM))
        outs["sbv_s"].append(sv_s.reshape(bs, ts, SB_HEADS, SB_HEAD_DIM))
        outs["wkv_s"].append(s_new)
        outs["shift_s"].append(proj_s[:, COL_RP:])

    st = {k: jnp.stack(v) for k, v in outs.items()}
    return (hp.reshape(bp, tp, D_MODEL), hs.reshape(bs, ts, D_MODEL),
            st["sbk_p"], st["sbv_p"], st["mk_p"], st["mv_p"], st["wkv_p"], st["shift_p"],
            st["sbk_s"], st["sbv_s"], st["wkv_s"], st["shift_s"])
```

```python
import functools

import numpy as np
import jax
import jax.numpy as jnp
from jax import lax
from jax.experimental import pallas as pl
from jax.experimental.pallas import tpu as pltpu

F32 = jnp.float32
BF16 = jnp.bfloat16

D_MODEL = 1024
SB_HEADS = 8
SB_HEAD_DIM = 64
SB_WIDTH = SB_HEADS * SB_HEAD_DIM
RW_HEADS = 8
RW_HEAD_DIM = 64
RW_WIDTH = RW_HEADS * RW_HEAD_DIM
RW_LORA = 64
RW_SHIFT_WIDTH = 3 * RW_WIDTH + 2 * RW_LORA
MEM_HEADS = 4
MEM_HEAD_DIM = 128
MEM_WIDTH = MEM_HEADS * MEM_HEAD_DIM
GN_EPS = 64e-5
RMS_EPS = 1e-6
C_IN = 4 * SB_WIDTH + RW_SHIFT_WIDTH + RW_WIDTH + 2 * MEM_WIDTH + 3 * D_MODEL

COL_GATE = 0
COL_SQ = 3 * D_MODEL
COL_SZ = COL_SQ + SB_WIDTH
COL_RZ = COL_SZ + SB_WIDTH
COL_MQ = COL_RZ + RW_WIDTH
COL_MZ = COL_MQ + MEM_WIDTH
ACT_WIDTH = COL_MZ + MEM_WIDTH
ACT_TILES = 2
assert ACT_WIDTH + RW_SHIFT_WIDTH + 2 * SB_WIDTH == C_IN and ACT_WIDTH % (ACT_TILES * 128) == 0
LOG2E = 1.4426950408889634
SB_KPAD = 128

VMEM_LIMIT_BYTES = 56 * 1024 * 1024
MXU_TILE = 256

RW_CHUNK = 64
RW_GROUP = MXU_TILE // RW_CHUNK
RW_NGROUP = RW_HEADS // RW_GROUP
RW_LOCKSTEP = 4
SB_SUB = MXU_TILE


def _cparams(n_grid):
    return pltpu.CompilerParams(dimension_semantics=("arbitrary",) * n_grid,
                                vmem_limit_bytes=VMEM_LIMIT_BYTES)


def _sigmoid(x):
    return 1.0 / (1.0 + jnp.exp(-x))


def _softplus(x):
    return jnp.maximum(x, 0.0) + jnp.log(1.0 + jnp.exp(-jnp.abs(x)))


def _silu(x):
    return x * _sigmoid(x)


def _split_bf16(x):
    hi = x.astype(BF16)
    lo = (x - hi.astype(F32)).astype(BF16)
    return hi, lo


def _dot(a, b):
    return jnp.dot(a, b, preferred_element_type=F32)


def _dot_nt(a, b):
    return lax.dot_general(a, b, (((1,), (1,)), ((), ())), preferred_element_type=F32)


def _dot_tn(a, b):
    return lax.dot_general(a, b, (((0,), (0,)), ((), ())), preferred_element_type=F32)


def _dot_hilo(x, w01):
    hi, lo = _split_bf16(x)
    return _dot(hi, w01) + _dot(lo, w01)


def _norm_proj_kernel(x_ref, g_ref, w_ref, o_ref, h_ref):
    @pl.when(pl.program_id(1) == 0)
    def _():
        x = x_ref[...]
        ms = jnp.mean(x * x, axis=-1, keepdims=True)
        h_ref[...] = (x * lax.rsqrt(ms + RMS_EPS) * g_ref[...]).astype(BF16)

    o_ref[...] = _dot(h_ref[...], w_ref[...])


def _norm_proj(x, g, w, tm, tn, name):
    t, d = x.shape
    n = w.shape[1]
    return pl.pallas_call(
        _norm_proj_kernel,
        grid=(t // tm, n // tn),
        in_specs=[pl.BlockSpec((tm, d), lambda i, j: (i, 0)),
                  pl.BlockSpec((1, d), lambda i, j: (0, 0)),
                  pl.BlockSpec((d, tn), lambda i, j: (0, j))],
        out_specs=pl.BlockSpec((tm, tn), lambda i, j: (i, j)),
        out_shape=jax.ShapeDtypeStruct((t, n), F32),
        scratch_shapes=[pltpu.VMEM((tm, d), BF16)],
        compiler_params=_cparams(2),
        name=name,
    )(x, g.reshape(1, d), w)


def _in_proj_kernel(x_ref, g_ref, wa_ref, wrp_ref, wkt_ref, wvt_ref, act_ref, rp_ref, kt_ref, vt_ref, h_ref):
    j = pl.program_id(1)

    @pl.when(j == 0)
    def _():
        x = x_ref[...]
        ms = jnp.mean(x * x, axis=-1, keepdims=True)
        h_ref[...] = (x * lax.rsqrt(ms + RMS_EPS) * g_ref[...]).astype(BF16)

    @pl.when(j < ACT_TILES)
    def _():
        act_ref[...] = _dot(h_ref[...], wa_ref[...])

    @pl.when(j == ACT_TILES)
    def _():
        rp_ref[...] = _dot(h_ref[...], wrp_ref[...])

    @pl.when(j == ACT_TILES + 1)
    def _():
        h = h_ref[...]
        kt_ref[...] = _dot_nt(wkt_ref[...], h)
        vt_ref[...] = _dot_nt(wvt_ref[...], h)


def _in_proj(x, g, w_act, w_rp, w_kt, w_vt, tm, name):
    t, d = x.shape
    tn = ACT_WIDTH // ACT_TILES
    last_act = ACT_TILES - 1
    const = lambda shape: pl.BlockSpec(shape, lambda i, j: (0, 0))
    return pl.pallas_call(
        _in_proj_kernel,
        grid=(t // tm, ACT_TILES + 2),
        in_specs=[pl.BlockSpec((tm, d), lambda i, j: (i, 0)), const((1, d)),
                  pl.BlockSpec((d, tn), lambda i, j: (0, jnp.minimum(j, last_act))),
                  const((d, RW_SHIFT_WIDTH)), const((SB_WIDTH, d)), const((SB_WIDTH, d))],
        out_specs=[pl.BlockSpec((tm, tn), lambda i, j: (i, jnp.minimum(j, last_act))),
                   pl.BlockSpec((tm, RW_SHIFT_WIDTH), lambda i, j: (i, 0)),
                   pl.BlockSpec((SB_WIDTH, tm), lambda i, j: (0, i)),
                   pl.BlockSpec((SB_WIDTH, tm), lambda i, j: (0, i))],
        out_shape=[jax.ShapeDtypeStruct((t, ACT_WIDTH), F32), jax.ShapeDtypeStruct((t, RW_SHIFT_WIDTH), F32),
                   jax.ShapeDtypeStruct((SB_WIDTH, t), F32), jax.ShapeDtypeStruct((SB_WIDTH, t), F32)],
        scratch_shapes=[pltpu.VMEM((tm, d), BF16)],
        compiler_params=_cparams(2),
        name=name,
    )(x, g.reshape(1, d), w_act, w_rp, w_kt, w_vt)


def _sb_prompt_kernel(qi_ref, kj_ref, bias_ref, q_ref, kt_ref, vt_ref, tri_ref, o_ref, q_scr, acc_ref, carry_ref,
                      *, bq, bk):
    s = pl.program_id(0)
    qi = qi_ref[s]
    kj = kj_ref[s]
    diag = kj == (qi * bq) // bk
    pad = SB_KPAD - SB_HEAD_DIM

    @pl.when(diag)
    def _():
        acc_ref[...] = jnp.zeros_like(acc_ref)
        carry_ref[...] = jnp.zeros_like(carry_ref)
        lane = lax.broadcasted_iota(jnp.int32, (bq, pad), 1)
        for h in range(SB_HEADS):
            qh = q_ref[:, h * SB_HEAD_DIM:(h + 1) * SB_HEAD_DIM] * (SB_HEAD_DIM ** -0.5 * LOG2E)
            ext = jnp.where(lane == 0, bias_ref[0, h],
                            jnp.where(lane == 1, bias_ref[1, h], jnp.where(lane == 2, bias_ref[2, h], 0.0)))
            q_scr[h] = jnp.concatenate([qh, ext], axis=1).astype(BF16)

    ones_rows = (lax.broadcasted_iota(jnp.int32, (pad, SB_SUB), 0) < 3).astype(BF16)

    def run(masked):
        def head(h, _):
            q = q_scr[h]
            tri = tri_ref[...]
            c = carry_ref[h]
            acc = acc_ref[h]
            for kb in reversed(range(bk // SB_SUB)):
                ks = slice(kb * SB_SUB, (kb + 1) * SB_SUB)
                kt = jnp.concatenate([kt_ref[h, :, ks].astype(BF16), ones_rows], axis=0)
                z = _dot(q, kt)
                neg_abs = pltpu.bitcast(pltpu.bitcast(z, jnp.uint32) | jnp.uint32(0x80000000), F32)
                sp = jnp.maximum(z, 0.0) + jnp.log(1.0 + jnp.exp2(neg_abs)) * LOG2E
                if masked:
                    row = lax.broadcasted_iota(jnp.int32, (bq, SB_SUB), 0) + qi * bq
                    col = lax.broadcasted_iota(jnp.int32, (bq, SB_SUB), 1) + (kj * bk + kb * SB_SUB)
                    vis = col < row
                    sp = jnp.where(vis, sp, 0.0)
                within = _dot(sp.astype(BF16), tri)
                w = jnp.exp2(z - sp - within - c)
                if masked:
                    w = jnp.where(vis, w, 0.0)
                acc = acc + _dot_nt(w.astype(BF16), vt_ref[h, :, ks].astype(BF16))
                c = c + jnp.sum(sp, axis=1, keepdims=True)
            carry_ref[h] = c
            acc_ref[h] = acc
            return 0

        lax.fori_loop(0, SB_HEADS, head, 0)

    @pl.when(diag)
    def _():
        run(True)

    @pl.when(jnp.logical_not(diag))
    def _():
        run(False)

    @pl.when(kj == 0)
    def _():
        for h in range(SB_HEADS):
            o_ref[:, h * SB_HEAD_DIM:(h + 1) * SB_HEAD_DIM] = acc_ref[h]


def _sb_prompt(act, kt, vt, bias, bq, bk):
    t = act.shape[0]
    h, d = SB_HEADS, SB_HEAD_DIM
    nq = t // bq
    assert bk % bq == 0 and t % bk == 0
    first = [(i * bq) // bk for i in range(nq)]
    qi = np.concatenate([np.full(first[i] + 1, i) for i in range(nq)]).astype(np.int32)
    kj = np.concatenate([np.arange(first[i], -1, -1) for i in range(nq)]).astype(np.int32)
    tri = (np.arange(SB_SUB)[:, None] > np.arange(SB_SUB)[None, :]).astype(np.float32)
    b2 = bias.astype(F32) * LOG2E
    b_hi = b2.astype(BF16).astype(F32)
    b_mid = (b2 - b_hi).astype(BF16).astype(F32)
    b_lo = (b2 - b_hi - b_mid).astype(BF16).astype(F32)
    grid_spec = pltpu.PrefetchScalarGridSpec(
        num_scalar_prefetch=2,
        grid=(len(qi),),
        in_specs=[pl.BlockSpec(memory_space=pltpu.SMEM),
                  pl.BlockSpec((bq, SB_WIDTH), lambda s, qi, kj: (qi[s], COL_SQ // SB_WIDTH)),
                  pl.BlockSpec((h, d, bk), lambda s, qi, kj: (0, 0, kj[s])),
                  pl.BlockSpec((h, d, bk), lambda s, qi, kj: (0, 0, kj[s])),
                  pl.BlockSpec((SB_SUB, SB_SUB), lambda s, qi, kj: (0, 0))],
        out_specs=pl.BlockSpec((bq, SB_WIDTH), lambda s, qi, kj: (qi[s], 0)),
        scratch_shapes=[pltpu.VMEM((h, bq, SB_KPAD), BF16), pltpu.VMEM((h, bq, d), F32),
                        pltpu.VMEM((h, bq, 1), F32)],
    )
    return pl.pallas_call(
        functools.partial(_sb_prompt_kernel, bq=bq, bk=bk),
        grid_spec=grid_spec,
        out_shape=jax.ShapeDtypeStruct((t, SB_WIDTH), F32),
        compiler_params=_cparams(1),
        name="sb_prompt",
    )(jnp.asarray(qi), jnp.asarray(kj), jnp.stack([b_hi, b_mid, b_lo]), act, kt, vt, jnp.asarray(tri, BF16))


def _sb_sample_kernel(pt_ref, q_ref, bias_ref, *refs, pp, n_steps):
    k_refs = refs[:pp]
    v_refs = refs[pp:2 * pp]
    tri_ref, o_ref, acc_ref, carry_ref, z_scr, w_scr = refs[2 * pp:]
    g = pl.program_id(1)

    @pl.when(g == 0)
    def _():
        acc_ref[...] = jnp.zeros_like(acc_ref)
        carry_ref[...] = jnp.zeros_like(carry_ref)

    q = q_ref[0]
    for i in range(pp):
        for h in range(SB_HEADS):
            z_scr[i * SB_HEADS + h:i * SB_HEADS + h + 1, :] = jnp.sum(k_refs[i][0, 0, h] * q[h], axis=0, keepdims=True)
    z = z_scr[...] + jnp.concatenate([bias_ref[...]] * pp, axis=0)
    sp = _softplus(z)
    within = _dot_hilo(sp, tri_ref[...])
    tot = jnp.sum(sp, axis=1, keepdims=True)
    c = carry_ref[...]
    cs = [None] * pp
    for i in reversed(range(pp)):
        cs[i] = c
        c = c + tot[i * SB_HEADS:(i + 1) * SB_HEADS]
    carry_ref[...] = c
    w_scr[...] = jnp.exp(z - sp - within - jnp.concatenate(cs, axis=0))
    for h in range(SB_HEADS):
        a = acc_ref[h]
        for i in range(pp):
            a = a + v_refs[i][0, 0, h] * w_scr[i * SB_HEADS + h:i * SB_HEADS + h + 1, :]
        acc_ref[h] = a

    @pl.when(g == n_steps - 1)
    def _():
        o_ref[0] = jnp.sum(acc_ref[...], axis=2, keepdims=True)


def _sb_sample(q, bias, k_pool, v_pool, layer, page_table, pp):
    b = q.shape[0]
    n_pages = page_table.shape[1]
    page = k_pool.shape[-1]
    n_steps = n_pages // pp
    tri = (np.arange(page)[:, None] > np.arange(page)[None, :]).astype(np.float32)

    def page_spec(i):
        return pl.BlockSpec((1, 1, SB_HEADS, SB_HEAD_DIM, page),
                            lambda bi, g, pt: (layer, pt[bi, (n_steps - 1 - g) * pp + i], 0, 0, 0))

    q_spec = pl.BlockSpec((1, SB_HEADS, SB_HEAD_DIM, 1), lambda bi, g, pt: (bi, 0, 0, 0))
    grid_spec = pltpu.PrefetchScalarGridSpec(
        num_scalar_prefetch=1,
        grid=(b, n_steps),
        in_specs=[q_spec, pl.BlockSpec((SB_HEADS, 1), lambda bi, g, pt: (0, 0))]
        + [page_spec(i) for i in range(pp)] + [page_spec(i) for i in range(pp)]
        + [pl.BlockSpec((page, page), lambda bi, g, pt: (0, 0))],
        out_specs=q_spec,
        scratch_shapes=[pltpu.VMEM((SB_HEADS, SB_HEAD_DIM, page), F32), pltpu.VMEM((SB_HEADS, 1), F32),
                        pltpu.VMEM((SB_HEADS * pp, page), F32), pltpu.VMEM((SB_HEADS * pp, page), F32)],
    )
    out = pl.pallas_call(
        functools.partial(_sb_sample_kernel, pp=pp, n_steps=n_steps),
        grid_spec=grid_spec,
        out_shape=jax.ShapeDtypeStruct((b, SB_HEADS, SB_HEAD_DIM, 1), F32),
        compiler_params=_cparams(2),
        name="sb_sample",
    )(page_table, q.reshape(b, SB_HEADS, SB_HEAD_DIM, 1), bias.reshape(SB_HEADS, 1),
      *([k_pool] * pp), *([v_pool] * pp), jnp.asarray(tri, BF16))
    return out.reshape(b, SB_WIDTH)


def _rw_prep(p, prev, mu, w0, a0, w2a, k_k, k_a, r_k, gsum):
    u = p + mu * (prev - p)
    r = u[:, 0:RW_WIDTH]
    k = u[:, RW_WIDTH:2 * RW_WIDTH]
    v = u[:, 2 * RW_WIDTH:3 * RW_WIDTH]
    x2 = u[:, 3 * RW_WIDTH:]
    lane = lax.broadcasted_iota(jnp.int32, x2.shape, 1)
    x2 = jnp.where(lane < RW_LORA, jnp.tanh(x2), x2)
    d = _dot(x2.astype(BF16), w2a)
    w_log = -_softplus(-(w0 + d[:, :RW_WIDTH])) - 0.5
    lw = -jnp.exp(w_log)
    a = _sigmoid(a0 + d[:, RW_WIDTH:])
    kk = k * k_k
    kk = kk / jnp.maximum(jnp.sqrt(_dot_hilo(kk * kk, gsum)), 1e-12)
    k = k * (1.0 + (a - 1.0) * k_a)
    bonus = _dot_hilo(r * k * r_k, gsum) * v
    return r, lw, k, v, kk, kk * a, bonus


def _neumann_inverse(a_list, eye, bdiag):
    x = [eye - a for a in a_list]
    p = [_dot(a.astype(BF16), bdiag(a)) for a in a_list]
    order = 2
    while 2 * order < RW_CHUNK:
        pb = [bdiag(pi) for pi in p]
        x = [xi + _dot(xi.astype(BF16), pbi) for xi, pbi in zip(x, pb)]
        p = [_dot(pi.astype(BF16), pbi) for pi, pbi in zip(p, pb)]
        order *= 2
    return [xi + _dot(xi.astype(BF16), bdiag(pi)) for xi, pi in zip(x, p)]


def _rwkv_prompt_kernel(p_ref, prev_ref, s0_ref, mu_ref, w0_ref, a0_ref, w2a_ref, kk_ref, ka_ref, rk_ref,
                        lng_ref, lnb_ref, gsum_ref, lincl_ref, ones_ref,
                        y_ref, sout_ref,
                        s_scr, last_scr, r_scr, lw_scr, k_scr, v_scr, kk_scr, b_scr, y_scr, *, tb):
    i = pl.program_id(0)

    @pl.when(i == 0)
    def _():
        s_scr[...] = s0_ref[...]
        last_scr[...] = prev_ref[...]

    p = p_ref[...]
    row0 = lax.broadcasted_iota(jnp.int32, p.shape, 0) == 0
    prev = jnp.where(row0, last_scr[...], pltpu.roll(p, 1, axis=0))
    last_scr[...] = p[tb - 1:tb, :]
    gsum = gsum_ref[...]
    r, lw, k, v, kk, b, bonus = _rw_prep(p, prev, mu_ref[...], w0_ref[...], a0_ref[...], w2a_ref[...],
                                         kk_ref[...], ka_ref[...], rk_ref[...], gsum)
    r_scr[...] = r
    lw_scr[...] = lw
    k_scr[...] = k
    v_scr[...] = v
    kk_scr[...] = kk
    b_scr[...] = b
    y_ref[...] = bonus

    n = RW_GROUP * RW_CHUNK
    row = lax.broadcasted_iota(jnp.int32, (n, n), 0)
    col = lax.broadcasted_iota(jnp.int32, (n, n), 1)
    same_head = (row // RW_CHUNK) == (col // RW_HEAD_DIM)
    trow = lax.broadcasted_iota(jnp.int32, (RW_CHUNK, n), 0)
    tcol = lax.broadcasted_iota(jnp.int32, (RW_CHUNK, n), 1) % RW_CHUNK
    strict = tcol < trow
    incl = tcol <= trow
    eye = (tcol == trow).astype(F32)

    def bdiag(x):
        return jnp.where(same_head, jnp.concatenate([x] * RW_GROUP, axis=0), 0.0).astype(BF16)

    groups = range(RW_NGROUP)
    sls = [slice(g * n, (g + 1) * n) for g in groups]

    def decayed(ci):
        rows = pl.ds(pl.multiple_of(ci * RW_CHUNK, RW_CHUNK), RW_CHUNK)
        lwc = lw_scr[rows, :]
        lhi, llo = _split_bf16(lwc)
        lincl = lincl_ref[...]
        cum = _dot(lincl, lhi) + _dot(lincl, llo)
        cum_end = cum[RW_CHUNK - 1:RW_CHUNK, :]
        ones = ones_ref[...]
        g_end_col = jnp.exp(_dot_tn(lhi, ones) + _dot_tn(llo, ones))
        g_inv = jnp.exp(-cum)
        g_out = jnp.exp(cum_end - cum)
        rc = r_scr[rows, :] * jnp.exp(cum)
        kkc = kk_scr[rows, :] * jnp.exp(cum - lwc)
        bc = b_scr[rows, :]
        kc = k_scr[rows, :]
        return dict(rows=rows, g_end_col=g_end_col, rc=rc, kkc=kkc, vc=v_scr[rows, :],
                    bt=bc * g_inv, kt=kc * g_inv, bh=bc * g_out, kh=kc * g_out)

    def chunks(ci, _):
        cs = [decayed(ci * RW_LOCKSTEP + j) for j in range(RW_LOCKSTEP)]
        probs = [(c, sl) for c in cs for sl in sls]
        kr = [jnp.concatenate([c["kkc"][:, sl], c["rc"][:, sl]], axis=0).astype(BF16) for c, sl in probs]
        m_b = [_dot_nt(kr[i], bdiag(c["bt"][:, sl])) for i, (c, sl) in enumerate(probs)]
        m_k = [_dot_nt(kr[i], bdiag(c["kt"][:, sl])) for i, (c, sl) in enumerate(probs)]
        t_inv = _neumann_inverse([jnp.where(strict, m[:RW_CHUNK], 0.0) for m in m_b], eye, bdiag)
        ap_k = [jnp.concatenate([jnp.where(strict, m[:RW_CHUNK], 0.0), jnp.where(incl, m[RW_CHUNK:], 0.0)],
                                axis=0).astype(BF16) for m in m_k]
        from_v = [_dot(ap_k[i], bdiag(c["vc"][:, sl])) for i, (c, sl) in enumerate(probs)]
        p_b = [jnp.where(incl, m[RW_CHUNK:], 0.0).astype(BF16) for m in m_b]
        for j, c in enumerate(cs):
            ids = [j * RW_NGROUP + g for g in groups]
            s_old = [s_scr[g] for g in groups]
            from_s = [_dot(kr[i], s_old[g].astype(BF16)) for g, i in zip(groups, ids)]
            u = [-_dot(t_inv[i].astype(BF16), bdiag(from_s[g][:RW_CHUNK] + from_v[i][:RW_CHUNK]))
                 for g, i in zip(groups, ids)]
            for g, i in zip(groups, ids):
                c_y = from_s[g][RW_CHUNK:] + _dot(p_b[i], bdiag(u[g])) + from_v[i][RW_CHUNK:]
                y_scr[c["rows"], sls[g]] = c_y
            upd = [_dot_tn(jnp.concatenate([c["bh"][:, sls[g]], c["kh"][:, sls[g]]], axis=0).astype(BF16),
                           jnp.concatenate([u[g], c["vc"][:, sls[g]]], axis=0).astype(BF16)) for g in groups]
            for g in groups:
                g_end = jnp.concatenate([c["g_end_col"][sls[g], :]] * (n // 128), axis=1)
                s_scr[g] = g_end * s_old[g] + jnp.where(same_head, upd[g], 0.0)
        return 0

    lax.fori_loop(0, tb // (RW_CHUNK * RW_LOCKSTEP), chunks, 0)

    y = y_scr[...]
    inv_n = 1.0 / RW_HEAD_DIM
    mean = _dot_hilo(y, gsum) * inv_n
    d = y - mean
    var = _dot_hilo(d * d, gsum) * inv_n
    y_ref[...] = d * lax.rsqrt(var + GN_EPS) * lng_ref[...] + lnb_ref[...] + y_ref[...]

    @pl.when(i == pl.num_programs(0) - 1)
    def _():
        sout_ref[...] = s_scr[...]


def _rw_consts():
    hd = np.arange(RW_WIDTH) // RW_HEAD_DIM
    gsum = (hd[:, None] == hd[None, :]).astype(np.float32)
    lincl = (np.arange(RW_CHUNK)[:, None] >= np.arange(RW_CHUNK)[None, :]).astype(np.float32)
    ones = np.ones((RW_CHUNK, 128), np.float32)
    return jnp.asarray(gsum, BF16), jnp.asarray(lincl, BF16), jnp.asarray(ones, BF16)


def _rw_params(lp):
    row = lambda a: a.reshape(1, -1)
    z = jnp.zeros((RW_LORA, RW_WIDTH), F32)
    w2a = jnp.concatenate([jnp.concatenate([lp["rw_w2"], z], axis=1),
                           jnp.concatenate([z, lp["rw_a2"]], axis=1)], axis=0).astype(BF16)
    return dict(mu=row(lp["shift_mu"]), w0=row(lp["rw_w0"]), a0=row(lp["rw_a0"]), w2a=w2a,
                k_k=row(lp["rw_k_k"]), k_a=row(lp["rw_k_a"]), r_k=row(lp["rw_r_k"]),
                ln_g=row(lp["rw_ln_g"]), ln_b=row(lp["rw_ln_b"]))


def _rwkv_prompt(proj, prev_row, s0_bd, rwp, tb):
    t = proj.shape[0]
    gsum, lincl, ones = _rw_consts()
    n = RW_GROUP * RW_CHUNK
    const = lambda shape: pl.BlockSpec(shape, lambda i: (0,) * len(shape))
    vec = const((1, RW_WIDTH))
    return pl.pallas_call(
        functools.partial(_rwkv_prompt_kernel, tb=tb),
        grid=(t // tb,),
        in_specs=[pl.BlockSpec((tb, RW_SHIFT_WIDTH), lambda i: (i, 0)),
                  const((1, RW_SHIFT_WIDTH)), const((RW_NGROUP, n, n)), const((1, RW_SHIFT_WIDTH)),
                  vec, vec, const((2 * RW_LORA, 2 * RW_WIDTH)), vec, vec, vec, vec, vec,
                  const((RW_WIDTH, RW_WIDTH)), const((RW_CHUNK, RW_CHUNK)), const((RW_CHUNK, 128))],
        out_specs=[pl.BlockSpec((tb, RW_WIDTH), lambda i: (i, 0)), const((RW_NGROUP, n, n))],
        out_shape=[jax.ShapeDtypeStruct((t, RW_WIDTH), F32), jax.ShapeDtypeStruct((RW_NGROUP, n, n), F32)],
        scratch_shapes=[pltpu.VMEM((RW_NGROUP, n, n), F32), pltpu.VMEM((1, RW_SHIFT_WIDTH), F32)]
        + [pltpu.VMEM((tb, RW_WIDTH), F32)] * 7,
        compiler_params=_cparams(1),
        name="rwkv_prompt",
    )(proj, prev_row, s0_bd, rwp["mu"], rwp["w0"], rwp["a0"], rwp["w2a"], rwp["k_k"], rwp["k_a"], rwp["r_k"],
      rwp["ln_g"], rwp["ln_b"], gsum, lincl, ones)


def _state_to_bd(s):
    st = jnp.swapaxes(s, -1, -2).reshape(RW_NGROUP, RW_GROUP, RW_HEAD_DIM, RW_HEAD_DIM)
    eye = jnp.eye(RW_GROUP, dtype=s.dtype)
    bd = st[:, :, :, None, :] * eye[None, :, None, :, None]
    n = RW_GROUP * RW_HEAD_DIM
    return bd.reshape(RW_NGROUP, n, n)


def _bd_to_state(bd):
    n = RW_GROUP * RW_HEAD_DIM
    b5 = bd.reshape(RW_NGROUP, RW_GROUP, RW_HEAD_DIM, RW_GROUP, RW_HEAD_DIM)
    idx = jnp.arange(RW_GROUP)
    blocks = b5[:, idx, :, idx, :]
    blocks = jnp.swapaxes(blocks, 0, 1).reshape(RW_HEADS, RW_HEAD_DIM, RW_HEAD_DIM)
    return jnp.swapaxes(blocks, -1, -2)


def _rwkv_sample_prep_kernel(p_ref, prev_ref, mu_ref, w0_ref, a0_ref, w2a_ref, kk_ref, ka_ref, rk_ref, gsum_ref,
                             r_ref, w_ref, k_ref, v_ref, kko_ref, b_ref, bonus_ref):
    r, lw, k, v, kk, b, bonus = _rw_prep(p_ref[...], prev_ref[...], mu_ref[...], w0_ref[...], a0_ref[...],
                                         w2a_ref[...], kk_ref[...], ka_ref[...], rk_ref[...], gsum_ref[...])
    r_ref[...] = r
    w_ref[...] = jnp.exp(lw)
    k_ref[...] = k
    v_ref[...] = v
    kko_ref[...] = kk
    b_ref[...] = b
    bonus_ref[...] = bonus


def _rwkv_sample_state_kernel(s_ref, r_ref, w_ref, k_ref, kk_ref, b_ref, v_ref, bonus_ref, lng_ref, lnb_ref,
                              y_ref, so_ref):
    s = s_ref[0]
    sa = jnp.sum(s * kk_ref[0], axis=-1, keepdims=True)
    s = s * w_ref[0] - sa * b_ref[0] + v_ref[0] * k_ref[0]
    so_ref[0] = s
    y = jnp.sum(s * r_ref[0], axis=-1, keepdims=True)
    mean = jnp.mean(y, axis=1, keepdims=True)
    d = y - mean
    var = jnp.mean(d * d, axis=1, keepdims=True)
    y_ref[0] = d * lax.rsqrt(var + GN_EPS) * lng_ref[...] + lnb_ref[...] + bonus_ref[0]


def _rwkv_sample(proj, prev_rows, state, rwp):
    b = proj.shape[0]
    gsum, _, _ = _rw_consts()
    full = lambda shape: pl.BlockSpec(shape, lambda i: (0,) * len(shape))
    vec = full((1, RW_WIDTH))
    outs = pl.pallas_call(
        _rwkv_sample_prep_kernel,
        grid=(1,),
        in_specs=[pl.BlockSpec((b, RW_SHIFT_WIDTH), lambda i: (0, 0)),
                  full((b, RW_SHIFT_WIDTH)), full((1, RW_SHIFT_WIDTH)), vec, vec,
                  full((2 * RW_LORA, 2 * RW_WIDTH)), vec, vec, vec, full((RW_WIDTH, RW_WIDTH))],
        out_specs=[full((b, RW_WIDTH))] * 7,
        out_shape=[jax.ShapeDtypeStruct((b, RW_WIDTH), F32)] * 7,
        compiler_params=_cparams(1),
        name="rwkv_sample_prep",
    )(proj, prev_rows, rwp["mu"], rwp["w0"], rwp["a0"], rwp["w2a"], rwp["k_k"], rwp["k_a"], rwp["r_k"], gsum)
    r, w, k, v, kk, bb, bonus = outs
    as_row = lambda a: a.reshape(b, RW_HEADS, 1, RW_HEAD_DIM)
    as_col = lambda a: a.reshape(-1, RW_HEADS, RW_HEAD_DIM, 1)
    row_spec = pl.BlockSpec((1, RW_HEADS, 1, RW_HEAD_DIM), lambda i: (i, 0, 0, 0))
    col_spec = pl.BlockSpec((1, RW_HEADS, RW_HEAD_DIM, 1), lambda i: (i, 0, 0, 0))
    par_spec = pl.BlockSpec((RW_HEADS, RW_HEAD_DIM, 1), lambda i: (0, 0, 0))
    s_spec = pl.BlockSpec((1, RW_HEADS, RW_HEAD_DIM, RW_HEAD_DIM), lambda i: (i, 0, 0, 0))
    y, s_new = pl.pallas_call(
        _rwkv_sample_state_kernel,
        grid=(b,),
        in_specs=[s_spec, row_spec, row_spec, row_spec, row_spec, row_spec, col_spec, col_spec, par_spec, par_spec],
        out_specs=[col_spec, s_spec],
        out_shape=[jax.ShapeDtypeStruct((b, RW_HEADS, RW_HEAD_DIM, 1), F32),
                   jax.ShapeDtypeStruct(state.shape, F32)],
        compiler_params=_cparams(1),
        name="rwkv_sample_state",
    )(state, as_row(r), as_row(w), as_row(k), as_row(kk), as_row(bb), as_col(v), as_col(bonus),
      rwp["ln_g"].reshape(RW_HEADS, RW_HEAD_DIM, 1), rwp["ln_b"].reshape(RW_HEADS, RW_HEAD_DIM, 1))
    return y.reshape(b, RW_WIDTH), s_new


def _mem_prompt_kernel(q_ref, mk_ref, mv_ref, o_ref):
    scale = MEM_HEAD_DIM ** -0.5
    for h in range(MEM_HEADS):
        sl = slice(h * MEM_HEAD_DIM, (h + 1) * MEM_HEAD_DIM)
        s = _dot_nt(q_ref[:, sl].astype(BF16), mk_ref[:, sl].astype(BF16)) * scale
        p = jnp.exp(s - jnp.max(s, axis=1, keepdims=True))
        l = jnp.sum(p, axis=1, keepdims=True)
        o_ref[:, sl] = _dot(p.astype(BF16), mv_ref[:, sl].astype(BF16)) / l


def _mem_prompt(proj, mk, mv, tm):
    t = proj.shape[0]
    m = mk.shape[0]
    return pl.pallas_call(
        _mem_prompt_kernel,
        grid=(t // tm,),
        in_specs=[pl.BlockSpec((tm, MEM_WIDTH), lambda i: (i, COL_MQ // MEM_WIDTH)),
                  pl.BlockSpec((m, MEM_WIDTH), lambda i: (0, 0)),
                  pl.BlockSpec((m, MEM_WIDTH), lambda i: (0, 0))],
        out_specs=pl.BlockSpec((tm, MEM_WIDTH), lambda i: (i, 0)),
        out_shape=jax.ShapeDtypeStruct((t, MEM_WIDTH), F32),
        compiler_params=_cparams(1),
        name="mem_prompt",
    )(proj, mk, mv)


def _mem_sample_kernel(q_ref, mk_ref, mv_ref, o_ref):
    scale = MEM_HEAD_DIM ** -0.5
    q = q_ref[0]
    rowh = lax.broadcasted_iota(jnp.int32, (8, MEM_WIDTH), 0)
    laneh = lax.broadcasted_iota(jnp.int32, (8, MEM_WIDTH), 1) // MEM_HEAD_DIM
    own = rowh == laneh
    qblk = jnp.where(own, q, 0.0).astype(BF16)
    s = _dot_nt(qblk, mk_ref[0].astype(BF16)) * scale
    p = jnp.exp(s - jnp.max(s, axis=1, keepdims=True))
    l = jnp.sum(p, axis=1, keepdims=True)
    o = _dot(p.astype(BF16), mv_ref[0].astype(BF16)) / l
    o_ref[0] = jnp.sum(jnp.where(own, o, 0.0), axis=0, keepdims=True)


def _mem_sample(q, mk, mv):
    b, m, _ = mk.shape
    q_spec = pl.BlockSpec((1, 1, MEM_WIDTH), lambda i: (i, 0, 0))
    kv_spec = pl.BlockSpec((1, m, MEM_WIDTH), lambda i: (i, 0, 0))
    out = pl.pallas_call(
        _mem_sample_kernel,
        grid=(b,),
        in_specs=[q_spec, kv_spec, kv_spec],
        out_specs=q_spec,
        out_shape=jax.ShapeDtypeStruct((b, 1, MEM_WIDTH), F32),
        compiler_params=_cparams(1),
        name="mem_sample",
    )(q.reshape(b, 1, MEM_WIDTH), mk, mv)
    return out.reshape(b, MEM_WIDTH)


def _merge_kernel(x_ref, gl_ref, osb_ref, sz_ref, orw_ref, rz_ref, omem_ref, mz_ref,
                  wsb_ref, wrw_ref, wmem_ref, wo_ref, fg_ref, o_ref, *, final):
    a_sb = (osb_ref[...] * _silu(sz_ref[...])).astype(BF16)
    a_rw = (orw_ref[...] * _silu(rz_ref[...])).astype(BF16)
    a_mem = (omem_ref[...] * _silu(mz_ref[...])).astype(BF16)
    merged = (_sigmoid(gl_ref[:, 0:D_MODEL]) * _dot(a_sb, wsb_ref[...])
              + _sigmoid(gl_ref[:, D_MODEL:2 * D_MODEL]) * _dot(a_rw, wrw_ref[...])
              + _sigmoid(gl_ref[:, 2 * D_MODEL:3 * D_MODEL]) * _dot(a_mem, wmem_ref[...]))
    y = x_ref[...] + _dot(merged.astype(BF16), wo_ref[...])
    if final:
        ms = jnp.mean(y * y, axis=-1, keepdims=True)
        y = y * lax.rsqrt(ms + RMS_EPS) * fg_ref[...]
    o_ref[...] = y


def _merge(x, proj, o_sb, o_rw, o_mem, w_sb, w_rw, w_mem, w_o, final_g, final, tm, name):
    t = x.shape[0]
    col = lambda c: pl.BlockSpec((tm, SB_WIDTH), lambda i: (i, c // SB_WIDTH))
    act = pl.BlockSpec((tm, SB_WIDTH), lambda i: (i, 0))
    wspec = pl.BlockSpec((SB_WIDTH, D_MODEL), lambda i: (0, 0))
    return pl.pallas_call(
        functools.partial(_merge_kernel, final=final),
        grid=(t // tm,),
        in_specs=[pl.BlockSpec((tm, D_MODEL), lambda i: (i, 0)),
                  pl.BlockSpec((tm, 3 * D_MODEL), lambda i: (i, 0)),
                  act, col(COL_SZ), act, col(COL_RZ), act, col(COL_MZ),
                  wspec, wspec, wspec, pl.BlockSpec((D_MODEL, D_MODEL), lambda i: (0, 0)),
                  pl.BlockSpec((1, D_MODEL), lambda i: (0, 0))],
        out_specs=pl.BlockSpec((tm, D_MODEL), lambda i: (i, 0)),
        out_shape=jax.ShapeDtypeStruct((t, D_MODEL), F32),
        compiler_params=_cparams(1),
        name=name,
    )(x, proj, o_sb, proj, o_rw, proj, o_mem, proj, w_sb, w_rw, w_mem, w_o, final_g.reshape(1, D_MODEL))


def _split_w_in(w_in):
    sq, sk, sv, sz, rp, rz, mq, mz, gl = jnp.split(
        w_in, np.cumsum([SB_WIDTH] * 4 + [RW_SHIFT_WIDTH, RW_WIDTH, MEM_WIDTH, MEM_WIDTH]).tolist(), axis=1)
    w_act = jnp.concatenate([gl, sq, sz, rz, mq, mz], axis=1).astype(BF16)
    return w_act, rp.astype(BF16), sk.T.astype(BF16), sv.T.astype(BF16)


def _pages_from_t(xt, page):
    t = xt.shape[1]
    x = xt.reshape(SB_HEADS, SB_HEAD_DIM, t // page, page)
    return jnp.transpose(x, (2, 3, 0, 1))[None]


def kernel(x_prompt, x_sample, cache_sb_k, cache_sb_v, cache_mem_k, cache_mem_v, state_wkv, state_shift, page_table, mem_prompt, norm_g, w_in, sb_bias, shift_mu, rw_w0, rw_w2, rw_a0, rw_a2, rw_k_k, rw_k_a, rw_r_k, rw_ln_g, rw_ln_b, mem_norm_g, w_mem_kv, w_bo_sb, w_bo_rw, w_bo_mem, w_o, final_norm_g):
    depth = w_in.shape[0]
    bp, tp, _ = x_prompt.shape
    bs, ts, _ = x_sample.shape
    assert bp == 1 and ts == 1
    page = cache_sb_k.shape[2]
    n_mem = mem_prompt.shape[1]
    sb_scale = SB_HEAD_DIM ** -0.5

    hp = x_prompt.reshape(tp, D_MODEL)
    hs = x_sample.reshape(bs, D_MODEL)
    mem = mem_prompt.reshape(n_mem, D_MODEL)
    k_pool_t = jnp.transpose(cache_sb_k, (0, 1, 3, 4, 2))
    v_pool_t = jnp.transpose(cache_sb_v, (0, 1, 3, 4, 2))
    outs = {k: [] for k in ("sbk_p", "sbv_p", "mk_p", "mv_p", "wkv_p", "shift_p", "sbk_s", "sbv_s", "wkv_s", "shift_s")}
    for l in range(depth):
        last = l == depth - 1
        lp = dict(shift_mu=shift_mu[l], rw_w0=rw_w0[l], rw_w2=rw_w2[l], rw_a0=rw_a0[l], rw_a2=rw_a2[l],
                  rw_k_k=rw_k_k[l], rw_k_a=rw_k_a[l], rw_r_k=rw_r_k[l], rw_ln_g=rw_ln_g[l], rw_ln_b=rw_ln_b[l])
        rwp = _rw_params(lp)
        w_act, w_rp, w_kt, w_vt = _split_w_in(w_in[l])
        w_sb, w_rw, w_mem, w_out = (w.astype(BF16) for w in (w_bo_sb[l], w_bo_rw[l], w_bo_mem[l], w_o[l]))

        act, rp, kt, vt = _in_proj(hp, norm_g[l], w_act, w_rp, w_kt, w_vt, 512, "proj_prompt")
        kv = _norm_proj(mem, mem_norm_g[l], w_mem_kv[l].astype(BF16), n_mem, MEM_WIDTH, "mem_kv")
        mk, mv = kv[:, :MEM_WIDTH], kv[:, MEM_WIDTH:]
        o_sb = _sb_prompt(act, kt.reshape(SB_HEADS, SB_HEAD_DIM, tp), vt.reshape(SB_HEADS, SB_HEAD_DIM, tp),
                          sb_bias[l], 512, 1024)
        s0 = _state_to_bd(jnp.zeros((RW_HEADS, RW_HEAD_DIM, RW_HEAD_DIM), F32))
        o_rw, s_bd = _rwkv_prompt(rp, jnp.zeros((1, RW_SHIFT_WIDTH), F32), s0, rwp, 512)
        o_mem = _mem_prompt(act, mk, mv, 512)
        hp = _merge(hp, act, o_sb, o_rw, o_mem, w_sb, w_rw, w_mem, w_out, final_norm_g, last, 256, "merge_prompt")
        outs["sbk_p"].append(_pages_from_t(kt, page))
        outs["sbv_p"].append(_pages_from_t(vt, page))
        outs["mk_p"].append(mk.reshape(bp, n_mem, MEM_HEADS, MEM_HEAD_DIM))
        outs["mv_p"].append(mv.reshape(bp, n_mem, MEM_HEADS, MEM_HEAD_DIM))
        outs["wkv_p"].append(_bd_to_state(s_bd).reshape(bp, RW_HEADS, RW_HEAD_DIM, RW_HEAD_DIM))
        outs["shift_p"].append(rp[tp - 1:tp])

        act_s, rp_s, kt_s, vt_s = _in_proj(hs, norm_g[l], w_act, w_rp, w_kt, w_vt, bs, "proj_sample")
        o_sb_s = _sb_sample(act_s[:, COL_SQ:COL_SQ + SB_WIDTH] * sb_scale, sb_bias[l], k_pool_t, v_pool_t, l,
                            page_table, 16)
        o_rw_s, s_new = _rwkv_sample(rp_s, state_shift[l], state_wkv[l], rwp)
        o_mem_s = _mem_sample(act_s[:, COL_MQ:COL_MQ + MEM_WIDTH],
                              cache_mem_k[l].reshape(bs, n_mem, MEM_WIDTH),
                              cache_mem_v[l].reshape(bs, n_mem, MEM_WIDTH))
        hs = _merge(hs, act_s, o_sb_s, o_rw_s, o_mem_s, w_sb, w_rw, w_mem, w_out, final_norm_g, last, bs,
                    "merge_sample")
        outs["sbk_s"].append(kt_s.T.reshape(bs, ts, SB_HEADS, SB_HEAD_DIM))
        outs["sbv_s"].append(vt_s.T.reshape(bs, ts, SB_HEADS, SB_HEAD_DIM))
        outs["wkv_s"].append(s_new)
        outs["shift_s"].append(rp_s)

    st = {k: jnp.stack(v) for k, v in outs.items()}
    return (hp.reshape(bp, tp, D_MODEL), hs.reshape(bs, ts, D_MODEL),
            st["sbk_p"], st["sbv_p"], st["mk_p"], st["mv_p"], st["wkv_p"], st["shift_p"],
            st["sbk_s"], st["sbv_s"], st["wkv_s"], st["shift_s"])
```

```python
import functools

import numpy as np
import jax
import jax.numpy as jnp
from jax import lax
from jax.experimental import pallas as pl
from jax.experimental.pallas import tpu as pltpu

F32 = jnp.float32
BF16 = jnp.bfloat16

D_MODEL = 1024
SB_HEADS = 8
SB_HEAD_DIM = 64
SB_WIDTH = SB_HEADS * SB_HEAD_DIM
RW_HEADS = 8
RW_HEAD_DIM = 64
RW_WIDTH = RW_HEADS * RW_HEAD_DIM
RW_LORA = 64
RW_SHIFT_WIDTH = 3 * RW_WIDTH + 2 * RW_LORA
MEM_HEADS = 4
MEM_HEAD_DIM = 128
MEM_WIDTH = MEM_HEADS * MEM_HEAD_DIM
GN_EPS = 64e-5
RMS_EPS = 1e-6
C_IN = 4 * SB_WIDTH + RW_SHIFT_WIDTH + RW_WIDTH + 2 * MEM_WIDTH + 3 * D_MODEL

COL_GATE = 0
COL_SQ = 3 * D_MODEL
COL_SZ = COL_SQ + SB_WIDTH
COL_RZ = COL_SZ + SB_WIDTH
COL_MQ = COL_RZ + RW_WIDTH
COL_MZ = COL_MQ + MEM_WIDTH
ACT_WIDTH = COL_MZ + MEM_WIDTH
ACT_TILES = 2
assert ACT_WIDTH + RW_SHIFT_WIDTH + 2 * SB_WIDTH == C_IN and ACT_WIDTH % (ACT_TILES * 128) == 0
LOG2E = 1.4426950408889634
SB_KPAD = 128

VMEM_LIMIT_BYTES = 56 * 1024 * 1024
MXU_TILE = 256

RW_CHUNK = 64
RW_GROUP = MXU_TILE // RW_CHUNK
RW_NGROUP = RW_HEADS // RW_GROUP
RW_LOCKSTEP = 4
SB_SUB = MXU_TILE


def _cparams(n_grid):
    return pltpu.CompilerParams(dimension_semantics=("arbitrary",) * n_grid,
                                vmem_limit_bytes=VMEM_LIMIT_BYTES)


def _sigmoid(x):
    return 1.0 / (1.0 + jnp.exp(-x))


def _softplus(x):
    return jnp.maximum(x, 0.0) + jnp.log(1.0 + jnp.exp(-jnp.abs(x)))


def _silu(x):
    return x * _sigmoid(x)


def _split_bf16(x):
    hi = x.astype(BF16)
    lo = (x - hi.astype(F32)).astype(BF16)
    return hi, lo


def _dot(a, b):
    return jnp.dot(a, b, preferred_element_type=F32)


def _dot_nt(a, b):
    return lax.dot_general(a, b, (((1,), (1,)), ((), ())), preferred_element_type=F32)


def _dot_tn(a, b):
    return lax.dot_general(a, b, (((0,), (0,)), ((), ())), preferred_element_type=F32)


def _dot_hilo(x, w01):
    hi, lo = _split_bf16(x)
    return _dot(hi, w01) + _dot(lo, w01)


def _norm_proj_kernel(x_ref, g_ref, w_ref, o_ref, h_ref):
    @pl.when(pl.program_id(1) == 0)
    def _():
        x = x_ref[...]
        ms = jnp.mean(x * x, axis=-1, keepdims=True)
        h_ref[...] = (x * lax.rsqrt(ms + RMS_EPS) * g_ref[...]).astype(BF16)

    o_ref[...] = _dot(h_ref[...], w_ref[...])


def _norm_proj(x, g, w, tm, tn, name):
    t, d = x.shape
    n = w.shape[1]
    return pl.pallas_call(
        _norm_proj_kernel,
        grid=(t // tm, n // tn),
        in_specs=[pl.BlockSpec((tm, d), lambda i, j: (i, 0)),
                  pl.BlockSpec((1, d), lambda i, j: (0, 0)),
                  pl.BlockSpec((d, tn), lambda i, j: (0, j))],
        out_specs=pl.BlockSpec((tm, tn), lambda i, j: (i, j)),
        out_shape=jax.ShapeDtypeStruct((t, n), F32),
        scratch_shapes=[pltpu.VMEM((tm, d), BF16)],
        compiler_params=_cparams(2),
        name=name,
    )(x, g.reshape(1, d), w)


def _in_proj_kernel(x_ref, g_ref, wa_ref, wrp_ref, wkt_ref, wvt_ref, act_ref, rp_ref, kt_ref, vt_ref, h_ref):
    j = pl.program_id(1)

    @pl.when(j == 0)
    def _():
        x = x_ref[...]
        ms = jnp.mean(x * x, axis=-1, keepdims=True)
        h_ref[...] = (x * lax.rsqrt(ms + RMS_EPS) * g_ref[...]).astype(BF16)

    @pl.when(j < ACT_TILES)
    def _():
        act_ref[...] = _dot(h_ref[...], wa_ref[...])

    @pl.when(j == ACT_TILES)
    def _():
        rp_ref[...] = _dot(h_ref[...], wrp_ref[...])

    @pl.when(j == ACT_TILES + 1)
    def _():
        h = h_ref[...]
        kt_ref[...] = _dot_nt(wkt_ref[...], h)
        vt_ref[...] = _dot_nt(wvt_ref[...], h)


def _in_proj(x, g, w_act, w_rp, w_kt, w_vt, tm, name):
    t, d = x.shape
    tn = ACT_WIDTH // ACT_TILES
    last_act = ACT_TILES - 1
    const = lambda shape: pl.BlockSpec(shape, lambda i, j: (0, 0))
    return pl.pallas_call(
        _in_proj_kernel,
        grid=(t // tm, ACT_TILES + 2),
        in_specs=[pl.BlockSpec((tm, d), lambda i, j: (i, 0)), const((1, d)),
                  pl.BlockSpec((d, tn), lambda i, j: (0, jnp.minimum(j, last_act))),
                  const((d, RW_SHIFT_WIDTH)), const((SB_WIDTH, d)), const((SB_WIDTH, d))],
        out_specs=[pl.BlockSpec((tm, tn), lambda i, j: (i, jnp.minimum(j, last_act))),
                   pl.BlockSpec((tm, RW_SHIFT_WIDTH), lambda i, j: (i, 0)),
                   pl.BlockSpec((SB_WIDTH, tm), lambda i, j: (0, i)),
                   pl.BlockSpec((SB_WIDTH, tm), lambda i, j: (0, i))],
        out_shape=[jax.ShapeDtypeStruct((t, ACT_WIDTH), F32), jax.ShapeDtypeStruct((t, RW_SHIFT_WIDTH), F32),
                   jax.ShapeDtypeStruct((SB_WIDTH, t), F32), jax.ShapeDtypeStruct((SB_WIDTH, t), F32)],
        scratch_shapes=[pltpu.VMEM((tm, d), BF16)],
        compiler_params=_cparams(2),
        name=name,
    )(x, g.reshape(1, d), w_act, w_rp, w_kt, w_vt)


def _sb_prompt_kernel(qi_ref, kj_ref, bias_ref, q_ref, kt_ref, vt_ref, tri_ref, o_ref, q_scr, acc_ref, carry_ref,
                      z_scr, *, bq, bk):
    s = pl.program_id(0)
    qi = qi_ref[s]
    kj = kj_ref[s]
    diag = kj == (qi * bq) // bk
    pad = SB_KPAD - SB_HEAD_DIM

    @pl.when(diag)
    def _():
        acc_ref[...] = jnp.zeros_like(acc_ref)
        carry_ref[...] = jnp.zeros_like(carry_ref)
        lane = lax.broadcasted_iota(jnp.int32, (bq, pad), 1)
        for h in range(SB_HEADS):
            qh = q_ref[:, h * SB_HEAD_DIM:(h + 1) * SB_HEAD_DIM] * (SB_HEAD_DIM ** -0.5 * LOG2E)
            ext = jnp.where(lane == 0, bias_ref[0, h],
                            jnp.where(lane == 1, bias_ref[1, h], jnp.where(lane == 2, bias_ref[2, h], 0.0)))
            q_scr[h] = jnp.concatenate([qh, ext], axis=1).astype(BF16)

    ones_rows = (lax.broadcasted_iota(jnp.int32, (pad, SB_SUB), 0) < 3).astype(BF16)

    def logits(h, kb):
        ks = slice(kb * SB_SUB, (kb + 1) * SB_SUB)
        return _dot(q_scr[h], jnp.concatenate([kt_ref[h, :, ks].astype(BF16), ones_rows], axis=0))

    def run(row_off):
        kbs = []
        for kb in reversed(range(bk // SB_SUB)):
            if row_off is None or (kb + 1) * SB_SUB <= row_off:
                kbs.append((kb, False))
            elif kb * SB_SUB < row_off + bq:
                kbs.append((kb, True))
        n_sub = len(kbs)
        assert n_sub % 2 == 0
        z_scr[0] = logits(0, kbs[0][0])

        def head(h, _):
            tri = tri_ref[...]
            c = carry_ref[h]
            acc = acc_ref[h]
            for i, (kb, masked) in enumerate(kbs):
                ks = slice(kb * SB_SUB, (kb + 1) * SB_SUB)
                z = z_scr[i % 2]
                z_scr[(i + 1) % 2] = (logits(h, kbs[i + 1][0]) if i + 1 < n_sub
                                      else logits((h + 1) % SB_HEADS, kbs[0][0]))
                neg_abs = pltpu.bitcast(pltpu.bitcast(z, jnp.uint32) | jnp.uint32(0x80000000), F32)
                sp = jnp.maximum(z, 0.0) + jnp.log(1.0 + jnp.exp2(neg_abs)) * LOG2E
                if masked:
                    row = lax.broadcasted_iota(jnp.int32, (bq, SB_SUB), 0) + row_off
                    col = lax.broadcasted_iota(jnp.int32, (bq, SB_SUB), 1) + kb * SB_SUB
                    vis = col < row
                    sp = jnp.where(vis, sp, 0.0)
                within = _dot(sp.astype(BF16), tri)
                w = jnp.exp2(z - sp - within - c)
                if masked:
                    w = jnp.where(vis, w, 0.0)
                acc = acc + _dot_nt(w.astype(BF16), vt_ref[h, :, ks].astype(BF16))
                c = c + jnp.sum(sp, axis=1, keepdims=True)
            carry_ref[h] = c
            acc_ref[h] = acc
            return 0

        lax.fori_loop(0, SB_HEADS, head, 0)

    for row_off in range(0, bk, bq):
        @pl.when(jnp.logical_and(diag, (qi * bq) % bk == row_off))
        def _():
            run(row_off)

    @pl.when(jnp.logical_not(diag))
    def _():
        run(None)

    @pl.when(kj == 0)
    def _():
        for h in range(SB_HEADS):
            o_ref[:, h * SB_HEAD_DIM:(h + 1) * SB_HEAD_DIM] = acc_ref[h]


def _sb_prompt(act, kt, vt, bias, bq, bk):
    t = act.shape[0]
    h, d = SB_HEADS, SB_HEAD_DIM
    nq = t // bq
    assert bk % bq == 0 and t % bk == 0
    first = [(i * bq) // bk for i in range(nq)]
    qi = np.concatenate([np.full(first[i] + 1, i) for i in range(nq)]).astype(np.int32)
    kj = np.concatenate([np.arange(first[i], -1, -1) for i in range(nq)]).astype(np.int32)
    tri = (np.arange(SB_SUB)[:, None] > np.arange(SB_SUB)[None, :]).astype(np.float32)
    b2 = bias.astype(F32) * LOG2E
    b_hi = b2.astype(BF16).astype(F32)
    b_mid = (b2 - b_hi).astype(BF16).astype(F32)
    b_lo = (b2 - b_hi - b_mid).astype(BF16).astype(F32)
    grid_spec = pltpu.PrefetchScalarGridSpec(
        num_scalar_prefetch=2,
        grid=(len(qi),),
        in_specs=[pl.BlockSpec(memory_space=pltpu.SMEM),
                  pl.BlockSpec((bq, SB_WIDTH), lambda s, qi, kj: (qi[s], COL_SQ // SB_WIDTH)),
                  pl.BlockSpec((h, d, bk), lambda s, qi, kj: (0, 0, kj[s])),
                  pl.BlockSpec((h, d, bk), lambda s, qi, kj: (0, 0, kj[s])),
                  pl.BlockSpec((SB_SUB, SB_SUB), lambda s, qi, kj: (0, 0))],
        out_specs=pl.BlockSpec((bq, SB_WIDTH), lambda s, qi, kj: (qi[s], 0)),
        scratch_shapes=[pltpu.VMEM((h, bq, SB_KPAD), BF16), pltpu.VMEM((h, bq, d), F32),
                        pltpu.VMEM((h, bq, 1), F32), pltpu.VMEM((2, bq, SB_SUB), F32)],
    )
    return pl.pallas_call(
        functools.partial(_sb_prompt_kernel, bq=bq, bk=bk),
        grid_spec=grid_spec,
        out_shape=jax.ShapeDtypeStruct((t, SB_WIDTH), F32),
        compiler_params=_cparams(1),
        name="sb_prompt",
    )(jnp.asarray(qi), jnp.asarray(kj), jnp.stack([b_hi, b_mid, b_lo]), act, kt, vt, jnp.asarray(tri, BF16))


def _sb_sample_kernel(pt_ref, q_ref, bias_ref, *refs, pp, n_steps):
    k_refs = refs[:pp]
    v_refs = refs[pp:2 * pp]
    tri_ref, o_ref, acc_ref, carry_ref, z_scr, w_scr = refs[2 * pp:]
    g = pl.program_id(1)

    @pl.when(g == 0)
    def _():
        acc_ref[...] = jnp.zeros_like(acc_ref)
        carry_ref[...] = jnp.zeros_like(carry_ref)

    q = q_ref[0]
    for i in range(pp):
        for h in range(SB_HEADS):
            z_scr[i * SB_HEADS + h:i * SB_HEADS + h + 1, :] = jnp.sum(k_refs[i][0, 0, h] * q[h], axis=0, keepdims=True)
    z = z_scr[...] + jnp.concatenate([bias_ref[...]] * pp, axis=0)
    sp = _softplus(z)
    within = _dot_hilo(sp, tri_ref[...])
    tot = jnp.sum(sp, axis=1, keepdims=True)
    c = carry_ref[...]
    cs = [None] * pp
    for i in reversed(range(pp)):
        cs[i] = c
        c = c + tot[i * SB_HEADS:(i + 1) * SB_HEADS]
    carry_ref[...] = c
    w_scr[...] = jnp.exp(z - sp - within - jnp.concatenate(cs, axis=0))
    for h in range(SB_HEADS):
        a = acc_ref[h]
        for i in range(pp):
            a = a + v_refs[i][0, 0, h] * w_scr[i * SB_HEADS + h:i * SB_HEADS + h + 1, :]
        acc_ref[h] = a

    @pl.when(g == n_steps - 1)
    def _():
        o_ref[0] = jnp.sum(acc_ref[...], axis=2, keepdims=True)


def _sb_sample(q, bias, k_pool, v_pool, layer, page_table, pp):
    b = q.shape[0]
    n_pages = page_table.shape[1]
    page = k_pool.shape[-1]
    n_steps = n_pages // pp
    tri = (np.arange(page)[:, None] > np.arange(page)[None, :]).astype(np.float32)

    def page_spec(i):
        return pl.BlockSpec((1, 1, SB_HEADS, SB_HEAD_DIM, page),
                            lambda bi, g, pt: (layer, pt[bi, (n_steps - 1 - g) * pp + i], 0, 0, 0))

    q_spec = pl.BlockSpec((1, SB_HEADS, SB_HEAD_DIM, 1), lambda bi, g, pt: (bi, 0, 0, 0))
    grid_spec = pltpu.PrefetchScalarGridSpec(
        num_scalar_prefetch=1,
        grid=(b, n_steps),
        in_specs=[q_spec, pl.BlockSpec((SB_HEADS, 1), lambda bi, g, pt: (0, 0))]
        + [page_spec(i) for i in range(pp)] + [page_spec(i) for i in range(pp)]
        + [pl.BlockSpec((page, page), lambda bi, g, pt: (0, 0))],
        out_specs=q_spec,
        scratch_shapes=[pltpu.VMEM((SB_HEADS, SB_HEAD_DIM, page), F32), pltpu.VMEM((SB_HEADS, 1), F32),
                        pltpu.VMEM((SB_HEADS * pp, page), F32), pltpu.VMEM((SB_HEADS * pp, page), F32)],
    )
    out = pl.pallas_call(
        functools.partial(_sb_sample_kernel, pp=pp, n_steps=n_steps),
        grid_spec=grid_spec,
        out_shape=jax.ShapeDtypeStruct((b, SB_HEADS, SB_HEAD_DIM, 1), F32),
        compiler_params=_cparams(2),
        name="sb_sample",
    )(page_table, q.reshape(b, SB_HEADS, SB_HEAD_DIM, 1), bias.reshape(SB_HEADS, 1),
      *([k_pool] * pp), *([v_pool] * pp), jnp.asarray(tri, BF16))
    return out.reshape(b, SB_WIDTH)


def _rw_prep(p, prev, mu, w0, a0, w2a, k_k, k_a, r_k, gsum):
    u = p + mu * (prev - p)
    r = u[:, 0:RW_WIDTH]
    k = u[:, RW_WIDTH:2 * RW_WIDTH]
    v = u[:, 2 * RW_WIDTH:3 * RW_WIDTH]
    x2 = u[:, 3 * RW_WIDTH:]
    lane = lax.broadcasted_iota(jnp.int32, x2.shape, 1)
    x2 = jnp.where(lane < RW_LORA, jnp.tanh(x2), x2)
    d = _dot(x2.astype(BF16), w2a)
    w_log = -_softplus(-(w0 + d[:, :RW_WIDTH])) - 0.5
    lw = -jnp.exp(w_log)
    a = _sigmoid(a0 + d[:, RW_WIDTH:])
    kk = k * k_k
    kk = kk / jnp.maximum(jnp.sqrt(_dot_hilo(kk * kk, gsum)), 1e-12)
    k = k * (1.0 + (a - 1.0) * k_a)
    bonus = _dot_hilo(r * k * r_k, gsum) * v
    return r, lw, k, v, kk, kk * a, bonus


def _neumann_inverse(a_list, eye, bdiag):
    x = [eye - a for a in a_list]
    p = [_dot(a.astype(BF16), bdiag(a)) for a in a_list]
    order = 2
    while 2 * order < RW_CHUNK:
        pb = [bdiag(pi) for pi in p]
        x = [xi + _dot(xi.astype(BF16), pbi) for xi, pbi in zip(x, pb)]
        p = [_dot(pi.astype(BF16), pbi) for pi, pbi in zip(p, pb)]
        order *= 2
    return [xi + _dot(xi.astype(BF16), bdiag(pi)) for xi, pi in zip(x, p)]


def _rwkv_prompt_kernel(p_ref, prev_ref, s0_ref, mu_ref, w0_ref, a0_ref, w2a_ref, kk_ref, ka_ref, rk_ref,
                        lng_ref, lnb_ref, gsum_ref, lincl_ref, ones_ref,
                        y_ref, sout_ref,
                        s_scr, last_scr, r_scr, lw_scr, k_scr, v_scr, kk_scr, b_scr, y_scr, *, tb):
    i = pl.program_id(0)

    @pl.when(i == 0)
    def _():
        s_scr[...] = s0_ref[...]
        last_scr[...] = prev_ref[...]

    p = p_ref[...]
    row0 = lax.broadcasted_iota(jnp.int32, p.shape, 0) == 0
    prev = jnp.where(row0, last_scr[...], pltpu.roll(p, 1, axis=0))
    last_scr[...] = p[tb - 1:tb, :]
    gsum = gsum_ref[...]
    r, lw, k, v, kk, b, bonus = _rw_prep(p, prev, mu_ref[...], w0_ref[...], a0_ref[...], w2a_ref[...],
                                         kk_ref[...], ka_ref[...], rk_ref[...], gsum)
    r_scr[...] = r
    lw_scr[...] = lw
    k_scr[...] = k
    v_scr[...] = v
    kk_scr[...] = kk
    b_scr[...] = b
    y_ref[...] = bonus

    n = RW_GROUP * RW_CHUNK
    row = lax.broadcasted_iota(jnp.int32, (n, n), 0)
    col = lax.broadcasted_iota(jnp.int32, (n, n), 1)
    same_head = (row // RW_CHUNK) == (col // RW_HEAD_DIM)
    trow = lax.broadcasted_iota(jnp.int32, (RW_CHUNK, n), 0)
    tcol = lax.broadcasted_iota(jnp.int32, (RW_CHUNK, n), 1) % RW_CHUNK
    strict = tcol < trow
    incl = tcol <= trow
    eye = (tcol == trow).astype(F32)

    def bdiag(x):
        return jnp.where(same_head, jnp.concatenate([x] * RW_GROUP, axis=0), 0.0).astype(BF16)

    groups = range(RW_NGROUP)
    sls = [slice(g * n, (g + 1) * n) for g in groups]

    def decayed(ci):
        rows = pl.ds(pl.multiple_of(ci * RW_CHUNK, RW_CHUNK), RW_CHUNK)
        lwc = lw_scr[rows, :]
        lhi, llo = _split_bf16(lwc)
        lincl = lincl_ref[...]
        cum = _dot(lincl, lhi) + _dot(lincl, llo)
        cum_end = cum[RW_CHUNK - 1:RW_CHUNK, :]
        ones = ones_ref[...]
        g_end_col = jnp.exp(_dot_tn(lhi, ones) + _dot_tn(llo, ones))
        g_inv = jnp.exp(-cum)
        g_out = jnp.exp(cum_end - cum)
        rc = r_scr[rows, :] * jnp.exp(cum)
        kkc = kk_scr[rows, :] * jnp.exp(cum - lwc)
        bc = b_scr[rows, :]
        kc = k_scr[rows, :]
        return dict(rows=rows, g_end_col=g_end_col, rc=rc, kkc=kkc, vc=v_scr[rows, :],
                    bt=bc * g_inv, kt=kc * g_inv, bh=bc * g_out, kh=kc * g_out)

    def chunks(ci, _):
        cs = [decayed(ci * RW_LOCKSTEP + j) for j in range(RW_LOCKSTEP)]
        probs = [(c, sl) for c in cs for sl in sls]
        kr = [jnp.concatenate([c["kkc"][:, sl], c["rc"][:, sl]], axis=0).astype(BF16) for c, sl in probs]
        m_b = [_dot_nt(kr[i], bdiag(c["bt"][:, sl])) for i, (c, sl) in enumerate(probs)]
        m_k = [_dot_nt(kr[i], bdiag(c["kt"][:, sl])) for i, (c, sl) in enumerate(probs)]
        t_inv = _neumann_inverse([jnp.where(strict, m[:RW_CHUNK], 0.0) for m in m_b], eye, bdiag)
        ap_k = [jnp.concatenate([jnp.where(strict, m[:RW_CHUNK], 0.0), jnp.where(incl, m[RW_CHUNK:], 0.0)],
                                axis=0).astype(BF16) for m in m_k]
        from_v = [_dot(ap_k[i], bdiag(c["vc"][:, sl])) for i, (c, sl) in enumerate(probs)]
        p_b = [jnp.where(incl, m[RW_CHUNK:], 0.0).astype(BF16) for m in m_b]
        for j, c in enumerate(cs):
            ids = [j * RW_NGROUP + g for g in groups]
            s_old = [s_scr[g] for g in groups]
            from_s = [_dot(kr[i], s_old[g].astype(BF16)) for g, i in zip(groups, ids)]
            u = [-_dot(t_inv[i].astype(BF16), bdiag(from_s[g][:RW_CHUNK] + from_v[i][:RW_CHUNK]))
                 for g, i in zip(groups, ids)]
            for g, i in zip(groups, ids):
                c_y = from_s[g][RW_CHUNK:] + _dot(p_b[i], bdiag(u[g])) + from_v[i][RW_CHUNK:]
                y_scr[c["rows"], sls[g]] = c_y
            upd = [_dot_tn(jnp.concatenate([c["bh"][:, sls[g]], c["kh"][:, sls[g]]], axis=0).astype(BF16),
                           jnp.concatenate([u[g], c["vc"][:, sls[g]]], axis=0).astype(BF16)) for g in groups]
            for g in groups:
                g_end = jnp.concatenate([c["g_end_col"][sls[g], :]] * (n // 128), axis=1)
                s_scr[g] = g_end * s_old[g] + jnp.where(same_head, upd[g], 0.0)
        return 0

    lax.fori_loop(0, tb // (RW_CHUNK * RW_LOCKSTEP), chunks, 0)

    y = y_scr[...]
    inv_n = 1.0 / RW_HEAD_DIM
    mean = _dot_hilo(y, gsum) * inv_n
    d = y - mean
    var = _dot_hilo(d * d, gsum) * inv_n
    y_ref[...] = d * lax.rsqrt(var + GN_EPS) * lng_ref[...] + lnb_ref[...] + y_ref[...]

    @pl.when(i == pl.num_programs(0) - 1)
    def _():
        sout_ref[...] = s_scr[...]


def _rw_consts():
    hd = np.arange(RW_WIDTH) // RW_HEAD_DIM
    gsum = (hd[:, None] == hd[None, :]).astype(np.float32)
    lincl = (np.arange(RW_CHUNK)[:, None] >= np.arange(RW_CHUNK)[None, :]).astype(np.float32)
    ones = np.ones((RW_CHUNK, 128), np.float32)
    return jnp.asarray(gsum, BF16), jnp.asarray(lincl, BF16), jnp.asarray(ones, BF16)


def _rw_params(lp):
    row = lambda a: a.reshape(1, -1)
    z = jnp.zeros((RW_LORA, RW_WIDTH), F32)
    w2a = jnp.concatenate([jnp.concatenate([lp["rw_w2"], z], axis=1),
                           jnp.concatenate([z, lp["rw_a2"]], axis=1)], axis=0).astype(BF16)
    return dict(mu=row(lp["shift_mu"]), w0=row(lp["rw_w0"]), a0=row(lp["rw_a0"]), w2a=w2a,
                k_k=row(lp["rw_k_k"]), k_a=row(lp["rw_k_a"]), r_k=row(lp["rw_r_k"]),
                ln_g=row(lp["rw_ln_g"]), ln_b=row(lp["rw_ln_b"]))


def _rwkv_prompt(proj, prev_row, s0_bd, rwp, tb):
    t = proj.shape[0]
    gsum, lincl, ones = _rw_consts()
    n = RW_GROUP * RW_CHUNK
    const = lambda shape: pl.BlockSpec(shape, lambda i: (0,) * len(shape))
    vec = const((1, RW_WIDTH))
    return pl.pallas_call(
        functools.partial(_rwkv_prompt_kernel, tb=tb),
        grid=(t // tb,),
        in_specs=[pl.BlockSpec((tb, RW_SHIFT_WIDTH), lambda i: (i, 0)),
                  const((1, RW_SHIFT_WIDTH)), const((RW_NGROUP, n, n)), const((1, RW_SHIFT_WIDTH)),
                  vec, vec, const((2 * RW_LORA, 2 * RW_WIDTH)), vec, vec, vec, vec, vec,
                  const((RW_WIDTH, RW_WIDTH)), const((RW_CHUNK, RW_CHUNK)), const((RW_CHUNK, 128))],
        out_specs=[pl.BlockSpec((tb, RW_WIDTH), lambda i: (i, 0)), const((RW_NGROUP, n, n))],
        out_shape=[jax.ShapeDtypeStruct((t, RW_WIDTH), F32), jax.ShapeDtypeStruct((RW_NGROUP, n, n), F32)],
        scratch_shapes=[pltpu.VMEM((RW_NGROUP, n, n), F32), pltpu.VMEM((1, RW_SHIFT_WIDTH), F32)]
        + [pltpu.VMEM((tb, RW_WIDTH), F32)] * 7,
        compiler_params=_cparams(1),
        name="rwkv_prompt",
    )(proj, prev_row, s0_bd, rwp["mu"], rwp["w0"], rwp["a0"], rwp["w2a"], rwp["k_k"], rwp["k_a"], rwp["r_k"],
      rwp["ln_g"], rwp["ln_b"], gsum, lincl, ones)


def _state_to_bd(s):
    st = jnp.swapaxes(s, -1, -2).reshape(RW_NGROUP, RW_GROUP, RW_HEAD_DIM, RW_HEAD_DIM)
    eye = jnp.eye(RW_GROUP, dtype=s.dtype)
    bd = st[:, :, :, None, :] * eye[None, :, None, :, None]
    n = RW_GROUP * RW_HEAD_DIM
    return bd.reshape(RW_NGROUP, n, n)


def _bd_to_state(bd):
    n = RW_GROUP * RW_HEAD_DIM
    b5 = bd.reshape(RW_NGROUP, RW_GROUP, RW_HEAD_DIM, RW_GROUP, RW_HEAD_DIM)
    idx = jnp.arange(RW_GROUP)
    blocks = b5[:, idx, :, idx, :]
    blocks = jnp.swapaxes(blocks, 0, 1).reshape(RW_HEADS, RW_HEAD_DIM, RW_HEAD_DIM)
    return jnp.swapaxes(blocks, -1, -2)


def _rwkv_sample_prep_kernel(p_ref, prev_ref, mu_ref, w0_ref, a0_ref, w2a_ref, kk_ref, ka_ref, rk_ref, gsum_ref,
                             r_ref, w_ref, k_ref, v_ref, kko_ref, b_ref, bonus_ref):
    r, lw, k, v, kk, b, bonus = _rw_prep(p_ref[...], prev_ref[...], mu_ref[...], w0_ref[...], a0_ref[...],
                                         w2a_ref[...], kk_ref[...], ka_ref[...], rk_ref[...], gsum_ref[...])
    r_ref[...] = r
    w_ref[...] = jnp.exp(lw)
    k_ref[...] = k
    v_ref[...] = v
    kko_ref[...] = kk
    b_ref[...] = b
    bonus_ref[...] = bonus


def _rwkv_sample_state_kernel(s_ref, r_ref, w_ref, k_ref, kk_ref, b_ref, v_ref, bonus_ref, lng_ref, lnb_ref,
                              y_ref, so_ref):
    s = s_ref[0]
    sa = jnp.sum(s * kk_ref[0], axis=-1, keepdims=True)
    s = s * w_ref[0] - sa * b_ref[0] + v_ref[0] * k_ref[0]
    so_ref[0] = s
    y = jnp.sum(s * r_ref[0], axis=-1, keepdims=True)
    mean = jnp.mean(y, axis=1, keepdims=True)
    d = y - mean
    var = jnp.mean(d * d, axis=1, keepdims=True)
    y_ref[0] = d * lax.rsqrt(var + GN_EPS) * lng_ref[...] + lnb_ref[...] + bonus_ref[0]


def _rwkv_sample(proj, prev_rows, state, rwp):
    b = proj.shape[0]
    gsum, _, _ = _rw_consts()
    full = lambda shape: pl.BlockSpec(shape, lambda i: (0,) * len(shape))
    vec = full((1, RW_WIDTH))
    outs = pl.pallas_call(
        _rwkv_sample_prep_kernel,
        grid=(1,),
        in_specs=[pl.BlockSpec((b, RW_SHIFT_WIDTH), lambda i: (0, 0)),
                  full((b, RW_SHIFT_WIDTH)), full((1, RW_SHIFT_WIDTH)), vec, vec,
                  full((2 * RW_LORA, 2 * RW_WIDTH)), vec, vec, vec, full((RW_WIDTH, RW_WIDTH))],
        out_specs=[full((b, RW_WIDTH))] * 7,
        out_shape=[jax.ShapeDtypeStruct((b, RW_WIDTH), F32)] * 7,
        compiler_params=_cparams(1),
        name="rwkv_sample_prep",
    )(proj, prev_rows, rwp["mu"], rwp["w0"], rwp["a0"], rwp["w2a"], rwp["k_k"], rwp["k_a"], rwp["r_k"], gsum)
    r, w, k, v, kk, bb, bonus = outs
    as_row = lambda a: a.reshape(b, RW_HEADS, 1, RW_HEAD_DIM)
    as_col = lambda a: a.reshape(-1, RW_HEADS, RW_HEAD_DIM, 1)
    row_spec = pl.BlockSpec((1, RW_HEADS, 1, RW_HEAD_DIM), lambda i: (i, 0, 0, 0))
    col_spec = pl.BlockSpec((1, RW_HEADS, RW_HEAD_DIM, 1), lambda i: (i, 0, 0, 0))
    par_spec = pl.BlockSpec((RW_HEADS, RW_HEAD_DIM, 1), lambda i: (0, 0, 0))
    s_spec = pl.BlockSpec((1, RW_HEADS, RW_HEAD_DIM, RW_HEAD_DIM), lambda i: (i, 0, 0, 0))
    y, s_new = pl.pallas_call(
        _rwkv_sample_state_kernel,
        grid=(b,),
        in_specs=[s_spec, row_spec, row_spec, row_spec, row_spec, row_spec, col_spec, col_spec, par_spec, par_spec],
        out_specs=[col_spec, s_spec],
        out_shape=[jax.ShapeDtypeStruct((b, RW_HEADS, RW_HEAD_DIM, 1), F32),
                   jax.ShapeDtypeStruct(state.shape, F32)],
        compiler_params=_cparams(1),
        name="rwkv_sample_state",
    )(state, as_row(r), as_row(w), as_row(k), as_row(kk), as_row(bb), as_col(v), as_col(bonus),
      rwp["ln_g"].reshape(RW_HEADS, RW_HEAD_DIM, 1), rwp["ln_b"].reshape(RW_HEADS, RW_HEAD_DIM, 1))
    return y.reshape(b, RW_WIDTH), s_new


def _mem_prompt_kernel(q_ref, mk_ref, mv_ref, o_ref):
    scale = MEM_HEAD_DIM ** -0.5
    for h in range(MEM_HEADS):
        sl = slice(h * MEM_HEAD_DIM, (h + 1) * MEM_HEAD_DIM)
        s = _dot_nt(q_ref[:, sl].astype(BF16), mk_ref[:, sl].astype(BF16)) * scale
        p = jnp.exp(s - jnp.max(s, axis=1, keepdims=True))
        l = jnp.sum(p, axis=1, keepdims=True)
        o_ref[:, sl] = _dot(p.astype(BF16), mv_ref[:, sl].astype(BF16)) / l


def _mem_prompt(proj, mk, mv, tm):
    t = proj.shape[0]
    m = mk.shape[0]
    return pl.pallas_call(
        _mem_prompt_kernel,
        grid=(t // tm,),
        in_specs=[pl.BlockSpec((tm, MEM_WIDTH), lambda i: (i, COL_MQ // MEM_WIDTH)),
                  pl.BlockSpec((m, MEM_WIDTH), lambda i: (0, 0)),
                  pl.BlockSpec((m, MEM_WIDTH), lambda i: (0, 0))],
        out_specs=pl.BlockSpec((tm, MEM_WIDTH), lambda i: (i, 0)),
        out_shape=jax.ShapeDtypeStruct((t, MEM_WIDTH), F32),
        compiler_params=_cparams(1),
        name="mem_prompt",
    )(proj, mk, mv)


def _mem_sample_kernel(q_ref, mk_ref, mv_ref, o_ref):
    scale = MEM_HEAD_DIM ** -0.5
    q = q_ref[0]
    rowh = lax.broadcasted_iota(jnp.int32, (8, MEM_WIDTH), 0)
    laneh = lax.broadcasted_iota(jnp.int32, (8, MEM_WIDTH), 1) // MEM_HEAD_DIM
    own = rowh == laneh
    qblk = jnp.where(own, q, 0.0).astype(BF16)
    s = _dot_nt(qblk, mk_ref[0].astype(BF16)) * scale
    p = jnp.exp(s - jnp.max(s, axis=1, keepdims=True))
    l = jnp.sum(p, axis=1, keepdims=True)
    o = _dot(p.astype(BF16), mv_ref[0].astype(BF16)) / l
    o_ref[0] = jnp.sum(jnp.where(own, o, 0.0), axis=0, keepdims=True)


def _mem_sample(q, mk, mv):
    b, m, _ = mk.shape
    q_spec = pl.BlockSpec((1, 1, MEM_WIDTH), lambda i: (i, 0, 0))
    kv_spec = pl.BlockSpec((1, m, MEM_WIDTH), lambda i: (i, 0, 0))
    out = pl.pallas_call(
        _mem_sample_kernel,
        grid=(b,),
        in_specs=[q_spec, kv_spec, kv_spec],
        out_specs=q_spec,
        out_shape=jax.ShapeDtypeStruct((b, 1, MEM_WIDTH), F32),
        compiler_params=_cparams(1),
        name="mem_sample",
    )(q.reshape(b, 1, MEM_WIDTH), mk, mv)
    return out.reshape(b, MEM_WIDTH)


def _merge_kernel(x_ref, gl_ref, osb_ref, sz_ref, orw_ref, rz_ref, omem_ref, mz_ref,
                  wsb_ref, wrw_ref, wmem_ref, wo_ref, fg_ref, o_ref, *, final):
    a_sb = (osb_ref[...] * _silu(sz_ref[...])).astype(BF16)
    a_rw = (orw_ref[...] * _silu(rz_ref[...])).astype(BF16)
    a_mem = (omem_ref[...] * _silu(mz_ref[...])).astype(BF16)
    merged = (_sigmoid(gl_ref[:, 0:D_MODEL]) * _dot(a_sb, wsb_ref[...])
              + _sigmoid(gl_ref[:, D_MODEL:2 * D_MODEL]) * _dot(a_rw, wrw_ref[...])
              + _sigmoid(gl_ref[:, 2 * D_MODEL:3 * D_MODEL]) * _dot(a_mem, wmem_ref[...]))
    y = x_ref[...] + _dot(merged.astype(BF16), wo_ref[...])
    if final:
        ms = jnp.mean(y * y, axis=-1, keepdims=True)
        y = y * lax.rsqrt(ms + RMS_EPS) * fg_ref[...]
    o_ref[...] = y


def _merge(x, proj, o_sb, o_rw, o_mem, w_sb, w_rw, w_mem, w_o, final_g, final, tm, name):
    t = x.shape[0]
    col = lambda c: pl.BlockSpec((tm, SB_WIDTH), lambda i: (i, c // SB_WIDTH))
    act = pl.BlockSpec((tm, SB_WIDTH), lambda i: (i, 0))
    wspec = pl.BlockSpec((SB_WIDTH, D_MODEL), lambda i: (0, 0))
    return pl.pallas_call(
        functools.partial(_merge_kernel, final=final),
        grid=(t // tm,),
        in_specs=[pl.BlockSpec((tm, D_MODEL), lambda i: (i, 0)),
                  pl.BlockSpec((tm, 3 * D_MODEL), lambda i: (i, 0)),
                  act, col(COL_SZ), act, col(COL_RZ), act, col(COL_MZ),
                  wspec, wspec, wspec, pl.BlockSpec((D_MODEL, D_MODEL), lambda i: (0, 0)),
                  pl.BlockSpec((1, D_MODEL), lambda i: (0, 0))],
        out_specs=pl.BlockSpec((tm, D_MODEL), lambda i: (i, 0)),
        out_shape=jax.ShapeDtypeStruct((t, D_MODEL), F32),
        compiler_params=_cparams(1),
        name=name,
    )(x, proj, o_sb, proj, o_rw, proj, o_mem, proj, w_sb, w_rw, w_mem, w_o, final_g.reshape(1, D_MODEL))


def _split_w_in(w_in):
    sq, sk, sv, sz, rp, rz, mq, mz, gl = jnp.split(
        w_in, np.cumsum([SB_WIDTH] * 4 + [RW_SHIFT_WIDTH, RW_WIDTH, MEM_WIDTH, MEM_WIDTH]).tolist(), axis=1)
    w_act = jnp.concatenate([gl, sq, sz, rz, mq, mz], axis=1).astype(BF16)
    return w_act, rp.astype(BF16), sk.T.astype(BF16), sv.T.astype(BF16)


def _pages_from_t(xt, page):
    t = xt.shape[1]
    x = xt.reshape(SB_HEADS, SB_HEAD_DIM, t // page, page)
    return jnp.transpose(x, (2, 3, 0, 1))[None]


def kernel(x_prompt, x_sample, cache_sb_k, cache_sb_v, cache_mem_k, cache_mem_v, state_wkv, state_shift, page_table, mem_prompt, norm_g, w_in, sb_bias, shift_mu, rw_w0, rw_w2, rw_a0, rw_a2, rw_k_k, rw_k_a, rw_r_k, rw_ln_g, rw_ln_b, mem_norm_g, w_mem_kv, w_bo_sb, w_bo_rw, w_bo_mem, w_o, final_norm_g):
    depth = w_in.shape[0]
    bp, tp, _ = x_prompt.shape
    bs, ts, _ = x_sample.shape
    assert bp == 1 and ts == 1
    page = cache_sb_k.shape[2]
    n_mem = mem_prompt.shape[1]
    sb_scale = SB_HEAD_DIM ** -0.5

    hp = x_prompt.reshape(tp, D_MODEL)
    hs = x_sample.reshape(bs, D_MODEL)
    mem = mem_prompt.reshape(n_mem, D_MODEL)
    k_pool_t = jnp.transpose(cache_sb_k, (0, 1, 3, 4, 2))
    v_pool_t = jnp.transpose(cache_sb_v, (0, 1, 3, 4, 2))
    outs = {k: [] for k in ("sbk_p", "sbv_p", "mk_p", "mv_p", "wkv_p", "shift_p", "sbk_s", "sbv_s", "wkv_s", "shift_s")}
    for l in range(depth):
        last = l == depth - 1
        lp = dict(shift_mu=shift_mu[l], rw_w0=rw_w0[l], rw_w2=rw_w2[l], rw_a0=rw_a0[l], rw_a2=rw_a2[l],
                  rw_k_k=rw_k_k[l], rw_k_a=rw_k_a[l], rw_r_k=rw_r_k[l], rw_ln_g=rw_ln_g[l], rw_ln_b=rw_ln_b[l])
        rwp = _rw_params(lp)
        w_act, w_rp, w_kt, w_vt = _split_w_in(w_in[l])
        w_sb, w_rw, w_mem, w_out = (w.astype(BF16) for w in (w_bo_sb[l], w_bo_rw[l], w_bo_mem[l], w_o[l]))

        act, rp, kt, vt = _in_proj(hp, norm_g[l], w_act, w_rp, w_kt, w_vt, 512, "proj_prompt")
        kv = _norm_proj(mem, mem_norm_g[l], w_mem_kv[l].astype(BF16), n_mem, MEM_WIDTH, "mem_kv")
        mk, mv = kv[:, :MEM_WIDTH], kv[:, MEM_WIDTH:]
        o_sb = _sb_prompt(act, kt.reshape(SB_HEADS, SB_HEAD_DIM, tp), vt.reshape(SB_HEADS, SB_HEAD_DIM, tp),
                          sb_bias[l], 512, 1024)
        s0 = _state_to_bd(jnp.zeros((RW_HEADS, RW_HEAD_DIM, RW_HEAD_DIM), F32))
        o_rw, s_bd = _rwkv_prompt(rp, jnp.zeros((1, RW_SHIFT_WIDTH), F32), s0, rwp, 512)
        o_mem = _mem_prompt(act, mk, mv, 512)
        hp = _merge(hp, act, o_sb, o_rw, o_mem, w_sb, w_rw, w_mem, w_out, final_norm_g, last, 256, "merge_prompt")
        outs["sbk_p"].append(_pages_from_t(kt, page))
        outs["sbv_p"].append(_pages_from_t(vt, page))
        outs["mk_p"].append(mk.reshape(bp, n_mem, MEM_HEADS, MEM_HEAD_DIM))
        outs["mv_p"].append(mv.reshape(bp, n_mem, MEM_HEADS, MEM_HEAD_DIM))
        outs["wkv_p"].append(_bd_to_state(s_bd).reshape(bp, RW_HEADS, RW_HEAD_DIM, RW_HEAD_DIM))
        outs["shift_p"].append(rp[tp - 1:tp])

        act_s, rp_s, kt_s, vt_s = _in_proj(hs, norm_g[l], w_act, w_rp, w_kt, w_vt, bs, "proj_sample")
        o_sb_s = _sb_sample(act_s[:, COL_SQ:COL_SQ + SB_WIDTH] * sb_scale, sb_bias[l], k_pool_t, v_pool_t, l,
                            page_table, 16)
        o_rw_s, s_new = _rwkv_sample(rp_s, state_shift[l], state_wkv[l], rwp)
        o_mem_s = _mem_sample(act_s[:, COL_MQ:COL_MQ + MEM_WIDTH],
                              cache_mem_k[l].reshape(bs, n_mem, MEM_WIDTH),
                              cache_mem_v[l].reshape(bs, n_mem, MEM_WIDTH))
        hs = _merge(hs, act_s, o_sb_s, o_rw_s, o_mem_s, w_sb, w_rw, w_mem, w_out, final_norm_g, last, bs,
                    "merge_sample")
        outs["sbk_s"].append(kt_s.T.reshape(bs, ts, SB_HEADS, SB_HEAD_DIM))
        outs["sbv_s"].append(vt_s.T.reshape(bs, ts, SB_HEADS, SB_HEAD_DIM))
        outs["wkv_s"].append(s_new)
        outs["shift_s"].append(rp_s)

    st = {k: jnp.stack(v) for k, v in outs.items()}
    return (hp.reshape(bp, tp, D_MODEL), hs.reshape(bs, ts, D_MODEL),
            st["sbk_p"], st["sbv_p"], st["mk_p"], st["mv_p"], st["wkv_p"], st["shift_p"],
            st["sbk_s"], st["sbv_s"], st["wkv_s"], st["shift_s"])
```

```python
import functools

import numpy as np
import jax
import jax.numpy as jnp
from jax import lax
from jax.experimental import pallas as pl
from jax.experimental.pallas import tpu as pltpu

F32 = jnp.float32
BF16 = jnp.bfloat16

D_MODEL = 1024
SB_HEADS = 8
SB_HEAD_DIM = 64
SB_WIDTH = SB_HEADS * SB_HEAD_DIM
RW_HEADS = 8
RW_HEAD_DIM = 64
RW_WIDTH = RW_HEADS * RW_HEAD_DIM
RW_LORA = 64
RW_SHIFT_WIDTH = 3 * RW_WIDTH + 2 * RW_LORA
MEM_HEADS = 4
MEM_HEAD_DIM = 128
MEM_WIDTH = MEM_HEADS * MEM_HEAD_DIM
GN_EPS = 64e-5
RMS_EPS = 1e-6
C_IN = 4 * SB_WIDTH + RW_SHIFT_WIDTH + RW_WIDTH + 2 * MEM_WIDTH + 3 * D_MODEL

COL_GATE = 0
COL_SQ = 3 * D_MODEL
COL_SZ = COL_SQ + SB_WIDTH
COL_RZ = COL_SZ + SB_WIDTH
COL_MQ = COL_RZ + RW_WIDTH
COL_MZ = COL_MQ + MEM_WIDTH
ACT_WIDTH = COL_MZ + MEM_WIDTH
ACT_TILES = 2
assert ACT_WIDTH + RW_SHIFT_WIDTH + 2 * SB_WIDTH == C_IN and ACT_WIDTH % (ACT_TILES * 128) == 0
LOG2E = 1.4426950408889634
SB_KPAD = 128

VMEM_LIMIT_BYTES = 56 * 1024 * 1024
MXU_TILE = 256

RW_CHUNK = 64
RW_GROUP = MXU_TILE // RW_CHUNK
RW_NGROUP = RW_HEADS // RW_GROUP
RW_LOCKSTEP = 4
SB_SUB = MXU_TILE
SAMPLE_ROWS_PER_STEP = 8


def _cparams(n_grid):
    return pltpu.CompilerParams(dimension_semantics=("arbitrary",) * n_grid,
                                vmem_limit_bytes=VMEM_LIMIT_BYTES)


def _sigmoid(x):
    return 1.0 / (1.0 + jnp.exp(-x))


def _softplus(x):
    return jnp.maximum(x, 0.0) + jnp.log(1.0 + jnp.exp(-jnp.abs(x)))


def _silu(x):
    return x * _sigmoid(x)


def _split_bf16(x):
    hi = x.astype(BF16)
    lo = (x - hi.astype(F32)).astype(BF16)
    return hi, lo


def _dot(a, b):
    return jnp.dot(a, b, preferred_element_type=F32)


def _dot_nt(a, b):
    return lax.dot_general(a, b, (((1,), (1,)), ((), ())), preferred_element_type=F32)


def _dot_tn(a, b):
    return lax.dot_general(a, b, (((0,), (0,)), ((), ())), preferred_element_type=F32)


def _dot_hilo(x, w01):
    hi, lo = _split_bf16(x)
    return _dot(hi, w01) + _dot(lo, w01)


def _norm_proj_kernel(x_ref, g_ref, w_ref, o_ref, h_ref):
    @pl.when(pl.program_id(1) == 0)
    def _():
        x = x_ref[...]
        ms = jnp.mean(x * x, axis=-1, keepdims=True)
        h_ref[...] = (x * lax.rsqrt(ms + RMS_EPS) * g_ref[...]).astype(BF16)

    o_ref[...] = _dot(h_ref[...], w_ref[...])


def _norm_proj(x, g, w, tm, tn, name):
    t, d = x.shape
    n = w.shape[1]
    return pl.pallas_call(
        _norm_proj_kernel,
        grid=(t // tm, n // tn),
        in_specs=[pl.BlockSpec((tm, d), lambda i, j: (i, 0)),
                  pl.BlockSpec((1, d), lambda i, j: (0, 0)),
                  pl.BlockSpec((d, tn), lambda i, j: (0, j))],
        out_specs=pl.BlockSpec((tm, tn), lambda i, j: (i, j)),
        out_shape=jax.ShapeDtypeStruct((t, n), F32),
        scratch_shapes=[pltpu.VMEM((tm, d), BF16)],
        compiler_params=_cparams(2),
        name=name,
    )(x, g.reshape(1, d), w)


def _in_proj_kernel(x_ref, g_ref, *refs):
    wa_refs = refs[:ACT_TILES]
    wrp_ref, wkt_ref, wvt_ref, act_ref, rp_ref, kt_ref, vt_ref, h_ref = refs[ACT_TILES:]
    j = pl.program_id(1)

    @pl.when(j == 0)
    def _():
        x = x_ref[...]
        ms = jnp.mean(x * x, axis=-1, keepdims=True)
        h_ref[...] = (x * lax.rsqrt(ms + RMS_EPS) * g_ref[...]).astype(BF16)

    for tile, wa_ref in enumerate(wa_refs):
        @pl.when(j == tile)
        def _():
            act_ref[...] = _dot(h_ref[...], wa_ref[...])

    @pl.when(j == ACT_TILES)
    def _():
        rp_ref[...] = _dot(h_ref[...], wrp_ref[...])

    @pl.when(j == ACT_TILES + 1)
    def _():
        h = h_ref[...]
        kt_ref[...] = _dot_nt(wkt_ref[...], h)
        vt_ref[...] = _dot_nt(wvt_ref[...], h)


def _in_proj(x, g, w_act, w_rp, w_kt, w_vt, tm, name):
    t, d = x.shape
    tn = ACT_WIDTH // ACT_TILES
    last_act = ACT_TILES - 1
    const = lambda shape: pl.BlockSpec(shape, lambda i, j: (0, 0), pipeline_mode=pl.Buffered(1))
    return pl.pallas_call(
        _in_proj_kernel,
        grid=(t // tm, ACT_TILES + 2),
        in_specs=[pl.BlockSpec((tm, d), lambda i, j: (i, 0)), const((1, d))]
        + [pl.BlockSpec((d, tn), lambda i, j, tile=tile: (0, tile), pipeline_mode=pl.Buffered(1))
           for tile in range(ACT_TILES)]
        + [const((d, RW_SHIFT_WIDTH)), const((SB_WIDTH, d)), const((SB_WIDTH, d))],
        out_specs=[pl.BlockSpec((tm, tn), lambda i, j: (i, jnp.minimum(j, last_act))),
                   pl.BlockSpec((tm, RW_SHIFT_WIDTH), lambda i, j: (i, 0)),
                   pl.BlockSpec((SB_WIDTH, tm), lambda i, j: (0, i)),
                   pl.BlockSpec((SB_WIDTH, tm), lambda i, j: (0, i))],
        out_shape=[jax.ShapeDtypeStruct((t, ACT_WIDTH), F32), jax.ShapeDtypeStruct((t, RW_SHIFT_WIDTH), F32),
                   jax.ShapeDtypeStruct((SB_WIDTH, t), F32), jax.ShapeDtypeStruct((SB_WIDTH, t), F32)],
        scratch_shapes=[pltpu.VMEM((tm, d), BF16)],
        compiler_params=_cparams(2),
        name=name,
    )(x, g.reshape(1, d), *([w_act] * ACT_TILES), w_rp, w_kt, w_vt)


def _sb_prompt_kernel(qi_ref, kj_ref, bias_ref, q_ref, kt_ref, vt_ref, tri_ref, o_ref, q_scr, acc_ref, carry_ref,
                      z_scr, *, bq, bk):
    s = pl.program_id(0)
    qi = qi_ref[s]
    kj = kj_ref[s]
    diag = kj == (qi * bq) // bk
    pad = SB_KPAD - SB_HEAD_DIM

    @pl.when(diag)
    def _():
        acc_ref[...] = jnp.zeros_like(acc_ref)
        carry_ref[...] = jnp.zeros_like(carry_ref)
        lane = lax.broadcasted_iota(jnp.int32, (bq, pad), 1)
        for h in range(SB_HEADS):
            qh = q_ref[:, h * SB_HEAD_DIM:(h + 1) * SB_HEAD_DIM] * (SB_HEAD_DIM ** -0.5 * LOG2E)
            ext = jnp.where(lane == 0, bias_ref[0, h],
                            jnp.where(lane == 1, bias_ref[1, h], jnp.where(lane == 2, bias_ref[2, h], 0.0)))
            q_scr[h] = jnp.concatenate([qh, ext], axis=1).astype(BF16)

    ones_rows = (lax.broadcasted_iota(jnp.int32, (pad, SB_SUB), 0) < 3).astype(BF16)

    def logits(h, kb):
        ks = slice(kb * SB_SUB, (kb + 1) * SB_SUB)
        return _dot(q_scr[h], jnp.concatenate([kt_ref[h, :, ks].astype(BF16), ones_rows], axis=0))

    def run(row_off):
        kbs = []
        for kb in reversed(range(bk // SB_SUB)):
            if row_off is None or (kb + 1) * SB_SUB <= row_off:
                kbs.append((kb, False))
            elif kb * SB_SUB < row_off + bq:
                kbs.append((kb, True))
        n_sub = len(kbs)
        assert n_sub % 2 == 0
        z_scr[0] = logits(0, kbs[0][0])

        def head(h, _):
            tri = tri_ref[...]
            c = carry_ref[h]
            acc = acc_ref[h]
            for i, (kb, masked) in enumerate(kbs):
                ks = slice(kb * SB_SUB, (kb + 1) * SB_SUB)
                z = z_scr[i % 2]
                z_scr[(i + 1) % 2] = (logits(h, kbs[i + 1][0]) if i + 1 < n_sub
                                      else logits((h + 1) % SB_HEADS, kbs[0][0]))
                neg_abs = pltpu.bitcast(pltpu.bitcast(z, jnp.uint32) | jnp.uint32(0x80000000), F32)
                sp = jnp.maximum(z, 0.0) + jnp.log(1.0 + jnp.exp2(neg_abs)) * LOG2E
                if masked:
                    row = lax.broadcasted_iota(jnp.int32, (bq, SB_SUB), 0) + row_off
                    col = lax.broadcasted_iota(jnp.int32, (bq, SB_SUB), 1) + kb * SB_SUB
                    vis = col < row
                    sp = jnp.where(vis, sp, 0.0)
                within = _dot(sp.astype(BF16), tri)
                w = jnp.exp2(z - sp - within - c)
                if masked:
                    w = jnp.where(vis, w, 0.0)
                acc = acc + _dot_nt(w.astype(BF16), vt_ref[h, :, ks].astype(BF16))
                c = c + jnp.sum(sp, axis=1, keepdims=True)
            carry_ref[h] = c
            acc_ref[h] = acc
            return 0

        lax.fori_loop(0, SB_HEADS, head, 0)

    for row_off in range(0, bk, bq):
        @pl.when(jnp.logical_and(diag, (qi * bq) % bk == row_off))
        def _():
            run(row_off)

    @pl.when(jnp.logical_not(diag))
    def _():
        run(None)

    @pl.when(kj == 0)
    def _():
        for h in range(SB_HEADS):
            o_ref[:, h * SB_HEAD_DIM:(h + 1) * SB_HEAD_DIM] = acc_ref[h]


def _sb_prompt(act, kt, vt, bias, bq, bk):
    t = act.shape[0]
    h, d = SB_HEADS, SB_HEAD_DIM
    nq = t // bq
    assert bk % bq == 0 and t % bk == 0
    first = [(i * bq) // bk for i in range(nq)]
    qi = np.concatenate([np.full(first[i] + 1, i) for i in range(nq)]).astype(np.int32)
    kj = np.concatenate([np.arange(first[i], -1, -1) for i in range(nq)]).astype(np.int32)
    tri = (np.arange(SB_SUB)[:, None] > np.arange(SB_SUB)[None, :]).astype(np.float32)
    b2 = bias.astype(F32) * LOG2E
    b_hi = b2.astype(BF16).astype(F32)
    b_mid = (b2 - b_hi).astype(BF16).astype(F32)
    b_lo = (b2 - b_hi - b_mid).astype(BF16).astype(F32)
    grid_spec = pltpu.PrefetchScalarGridSpec(
        num_scalar_prefetch=2,
        grid=(len(qi),),
        in_specs=[pl.BlockSpec(memory_space=pltpu.SMEM),
                  pl.BlockSpec((bq, SB_WIDTH), lambda s, qi, kj: (qi[s], COL_SQ // SB_WIDTH)),
                  pl.BlockSpec((h, d, bk), lambda s, qi, kj: (0, 0, kj[s])),
                  pl.BlockSpec((h, d, bk), lambda s, qi, kj: (0, 0, kj[s])),
                  pl.BlockSpec((SB_SUB, SB_SUB), lambda s, qi, kj: (0, 0))],
        out_specs=pl.BlockSpec((bq, SB_WIDTH), lambda s, qi, kj: (qi[s], 0)),
        scratch_shapes=[pltpu.VMEM((h, bq, SB_KPAD), BF16), pltpu.VMEM((h, bq, d), F32),
                        pltpu.VMEM((h, bq, 1), F32), pltpu.VMEM((2, bq, SB_SUB), F32)],
    )
    return pl.pallas_call(
        functools.partial(_sb_prompt_kernel, bq=bq, bk=bk),
        grid_spec=grid_spec,
        out_shape=jax.ShapeDtypeStruct((t, SB_WIDTH), F32),
        compiler_params=_cparams(1),
        name="sb_prompt",
    )(jnp.asarray(qi), jnp.asarray(kj), jnp.stack([b_hi, b_mid, b_lo]), act, kt, vt, jnp.asarray(tri, BF16))


def _sb_sample_kernel(pt_ref, q_ref, bias_ref, *refs, pp, n_steps):
    k_refs = refs[:pp]
    v_refs = refs[pp:2 * pp]
    tri_ref, o_ref, acc_ref, carry_ref, z_scr, w_scr = refs[2 * pp:]
    g = pl.program_id(1)

    @pl.when(g == 0)
    def _():
        acc_ref[...] = jnp.zeros_like(acc_ref)
        carry_ref[...] = jnp.zeros_like(carry_ref)

    q = q_ref[0]
    for i in range(pp):
        for h in range(SB_HEADS):
            z_scr[i * SB_HEADS + h:i * SB_HEADS + h + 1, :] = jnp.sum(k_refs[i][0, 0, h] * q[h], axis=0, keepdims=True)
    z = z_scr[...] + jnp.concatenate([bias_ref[...]] * pp, axis=0)
    sp = _softplus(z)
    within = _dot_hilo(sp, tri_ref[...])
    tot = jnp.sum(sp, axis=1, keepdims=True)
    c = carry_ref[...]
    cs = [None] * pp
    for i in reversed(range(pp)):
        cs[i] = c
        c = c + tot[i * SB_HEADS:(i + 1) * SB_HEADS]
    carry_ref[...] = c
    w_scr[...] = jnp.exp(z - sp - within - jnp.concatenate(cs, axis=0))
    for h in range(SB_HEADS):
        a = acc_ref[h]
        for i in range(pp):
            a = a + v_refs[i][0, 0, h] * w_scr[i * SB_HEADS + h:i * SB_HEADS + h + 1, :]
        acc_ref[h] = a

    @pl.when(g == n_steps - 1)
    def _():
        o_ref[0] = jnp.sum(acc_ref[...], axis=2, keepdims=True)


def _sb_sample(q, bias, k_pool, v_pool, layer, page_table, pp):
    b = q.shape[0]
    n_pages = page_table.shape[1]
    page = k_pool.shape[-1]
    n_steps = n_pages // pp
    tri = (np.arange(page)[:, None] > np.arange(page)[None, :]).astype(np.float32)

    def page_spec(i):
        return pl.BlockSpec((1, 1, SB_HEADS, SB_HEAD_DIM, page),
                            lambda bi, g, pt: (layer, pt[bi, (n_steps - 1 - g) * pp + i], 0, 0, 0))

    q_spec = pl.BlockSpec((1, SB_HEADS, SB_HEAD_DIM, 1), lambda bi, g, pt: (bi, 0, 0, 0))
    grid_spec = pltpu.PrefetchScalarGridSpec(
        num_scalar_prefetch=1,
        grid=(b, n_steps),
        in_specs=[q_spec, pl.BlockSpec((SB_HEADS, 1), lambda bi, g, pt: (0, 0))]
        + [page_spec(i) for i in range(pp)] + [page_spec(i) for i in range(pp)]
        + [pl.BlockSpec((page, page), lambda bi, g, pt: (0, 0))],
        out_specs=q_spec,
        scratch_shapes=[pltpu.VMEM((SB_HEADS, SB_HEAD_DIM, page), F32), pltpu.VMEM((SB_HEADS, 1), F32),
                        pltpu.VMEM((SB_HEADS * pp, page), F32), pltpu.VMEM((SB_HEADS * pp, page), F32)],
    )
    out = pl.pallas_call(
        functools.partial(_sb_sample_kernel, pp=pp, n_steps=n_steps),
        grid_spec=grid_spec,
        out_shape=jax.ShapeDtypeStruct((b, SB_HEADS, SB_HEAD_DIM, 1), F32),
        compiler_params=_cparams(2),
        name="sb_sample",
    )(page_table, q.reshape(b, SB_HEADS, SB_HEAD_DIM, 1), bias.reshape(SB_HEADS, 1),
      *([k_pool] * pp), *([v_pool] * pp), jnp.asarray(tri, BF16))
    return out.reshape(b, SB_WIDTH)


def _rw_prep(p, prev, mu, w0, a0, w2a, k_k, k_a, r_k, gsum):
    u = p + mu * (prev - p)
    r = u[:, 0:RW_WIDTH]
    k = u[:, RW_WIDTH:2 * RW_WIDTH]
    v = u[:, 2 * RW_WIDTH:3 * RW_WIDTH]
    x2 = u[:, 3 * RW_WIDTH:]
    lane = lax.broadcasted_iota(jnp.int32, x2.shape, 1)
    x2 = jnp.where(lane < RW_LORA, jnp.tanh(x2), x2)
    d = _dot(x2.astype(BF16), w2a)
    w_log = -_softplus(-(w0 + d[:, :RW_WIDTH])) - 0.5
    lw = -jnp.exp(w_log)
    a = _sigmoid(a0 + d[:, RW_WIDTH:])
    kk = k * k_k
    kk = kk / jnp.maximum(jnp.sqrt(_dot_hilo(kk * kk, gsum)), 1e-12)
    k = k * (1.0 + (a - 1.0) * k_a)
    bonus = _dot_hilo(r * k * r_k, gsum) * v
    return r, lw, k, v, kk, kk * a, bonus


def _neumann_inverse(a_list, eye, bdiag):
    x = [eye - a for a in a_list]
    p = [_dot(a.astype(BF16), bdiag(a)) for a in a_list]
    order = 2
    while 2 * order < RW_CHUNK:
        pb = [bdiag(pi) for pi in p]
        x = [xi + _dot(xi.astype(BF16), pbi) for xi, pbi in zip(x, pb)]
        p = [_dot(pi.astype(BF16), pbi) for pi, pbi in zip(p, pb)]
        order *= 2
    return [xi + _dot(xi.astype(BF16), bdiag(pi)) for xi, pi in zip(x, p)]


def _rwkv_prompt_kernel(p_ref, prev_ref, s0_ref, mu_ref, w0_ref, a0_ref, w2a_ref, kk_ref, ka_ref, rk_ref,
                        lng_ref, lnb_ref, gsum_ref, lincl_ref, ones_ref,
                        y_ref, sout_ref,
                        s_scr, last_scr, r_scr, lw_scr, k_scr, v_scr, kk_scr, b_scr, y_scr, *, tb):
    i = pl.program_id(0)

    @pl.when(i == 0)
    def _():
        s_scr[...] = s0_ref[...]
        last_scr[...] = prev_ref[...]

    p = p_ref[...]
    row0 = lax.broadcasted_iota(jnp.int32, p.shape, 0) == 0
    prev = jnp.where(row0, last_scr[...], pltpu.roll(p, 1, axis=0))
    last_scr[...] = p[tb - 1:tb, :]
    gsum = gsum_ref[...]
    r, lw, k, v, kk, b, bonus = _rw_prep(p, prev, mu_ref[...], w0_ref[...], a0_ref[...], w2a_ref[...],
                                         kk_ref[...], ka_ref[...], rk_ref[...], gsum)
    r_scr[...] = r
    lw_scr[...] = lw
    k_scr[...] = k
    v_scr[...] = v
    kk_scr[...] = kk
    b_scr[...] = b
    y_ref[...] = bonus

    n = RW_GROUP * RW_CHUNK
    row = lax.broadcasted_iota(jnp.int32, (n, n), 0)
    col = lax.broadcasted_iota(jnp.int32, (n, n), 1)
    same_head = (row // RW_CHUNK) == (col // RW_HEAD_DIM)
    trow = lax.broadcasted_iota(jnp.int32, (RW_CHUNK, n), 0)
    tcol = lax.broadcasted_iota(jnp.int32, (RW_CHUNK, n), 1) % RW_CHUNK
    strict = tcol < trow
    incl = tcol <= trow
    eye = (tcol == trow).astype(F32)

    def bdiag(x):
        return jnp.where(same_head, jnp.concatenate([x] * RW_GROUP, axis=0), 0.0).astype(BF16)

    groups = range(RW_NGROUP)
    sls = [slice(g * n, (g + 1) * n) for g in groups]

    def decayed(ci):
        rows = pl.ds(pl.multiple_of(ci * RW_CHUNK, RW_CHUNK), RW_CHUNK)
        lwc = lw_scr[rows, :]
        lhi, llo = _split_bf16(lwc)
        lincl = lincl_ref[...]
        cum = _dot(lincl, lhi) + _dot(lincl, llo)
        cum_end = cum[RW_CHUNK - 1:RW_CHUNK, :]
        ones = ones_ref[...]
        g_end_col = jnp.exp(_dot_tn(lhi, ones) + _dot_tn(llo, ones))
        g_inv = jnp.exp(-cum)
        g_out = jnp.exp(cum_end - cum)
        rc = r_scr[rows, :] * jnp.exp(cum)
        kkc = kk_scr[rows, :] * jnp.exp(cum - lwc)
        bc = b_scr[rows, :]
        kc = k_scr[rows, :]
        return dict(rows=rows, g_end_col=g_end_col, rc=rc, kkc=kkc, vc=v_scr[rows, :],
                    bt=bc * g_inv, kt=kc * g_inv, bh=bc * g_out, kh=kc * g_out)

    def chunks(ci, _):
        cs = [decayed(ci * RW_LOCKSTEP + j) for j in range(RW_LOCKSTEP)]
        probs = [(c, sl) for c in cs for sl in sls]
        kr = [jnp.concatenate([c["kkc"][:, sl], c["rc"][:, sl]], axis=0).astype(BF16) for c, sl in probs]
        m_b = [_dot_nt(kr[i], bdiag(c["bt"][:, sl])) for i, (c, sl) in enumerate(probs)]
        m_k = [_dot_nt(kr[i], bdiag(c["kt"][:, sl])) for i, (c, sl) in enumerate(probs)]
        t_inv = _neumann_inverse([jnp.where(strict, m[:RW_CHUNK], 0.0) for m in m_b], eye, bdiag)
        ap_k = [jnp.concatenate([jnp.where(strict, m[:RW_CHUNK], 0.0), jnp.where(incl, m[RW_CHUNK:], 0.0)],
                                axis=0).astype(BF16) for m in m_k]
        from_v = [_dot(ap_k[i], bdiag(c["vc"][:, sl])) for i, (c, sl) in enumerate(probs)]
        p_b = [jnp.where(incl, m[RW_CHUNK:], 0.0).astype(BF16) for m in m_b]
        for j, c in enumerate(cs):
            ids = [j * RW_NGROUP + g for g in groups]
            s_old = [s_scr[g] for g in groups]
            from_s = [_dot(kr[i], s_old[g].astype(BF16)) for g, i in zip(groups, ids)]
            u = [-_dot(t_inv[i].astype(BF16), bdiag(from_s[g][:RW_CHUNK] + from_v[i][:RW_CHUNK]))
                 for g, i in zip(groups, ids)]
            for g, i in zip(groups, ids):
                c_y = from_s[g][RW_CHUNK:] + _dot(p_b[i], bdiag(u[g])) + from_v[i][RW_CHUNK:]
                y_scr[c["rows"], sls[g]] = c_y
            upd = [_dot_tn(jnp.concatenate([c["bh"][:, sls[g]], c["kh"][:, sls[g]]], axis=0).astype(BF16),
                           jnp.concatenate([u[g], c["vc"][:, sls[g]]], axis=0).astype(BF16)) for g in groups]
            for g in groups:
                g_end = jnp.concatenate([c["g_end_col"][sls[g], :]] * (n // 128), axis=1)
                s_scr[g] = g_end * s_old[g] + jnp.where(same_head, upd[g], 0.0)
        return 0

    lax.fori_loop(0, tb // (RW_CHUNK * RW_LOCKSTEP), chunks, 0)

    y = y_scr[...]
    inv_n = 1.0 / RW_HEAD_DIM
    mean = _dot_hilo(y, gsum) * inv_n
    d = y - mean
    var = _dot_hilo(d * d, gsum) * inv_n
    y_ref[...] = d * lax.rsqrt(var + GN_EPS) * lng_ref[...] + lnb_ref[...] + y_ref[...]

    @pl.when(i == pl.num_programs(0) - 1)
    def _():
        sout_ref[...] = s_scr[...]


def _rw_consts():
    hd = np.arange(RW_WIDTH) // RW_HEAD_DIM
    gsum = (hd[:, None] == hd[None, :]).astype(np.float32)
    lincl = (np.arange(RW_CHUNK)[:, None] >= np.arange(RW_CHUNK)[None, :]).astype(np.float32)
    ones = np.ones((RW_CHUNK, 128), np.float32)
    return jnp.asarray(gsum, BF16), jnp.asarray(lincl, BF16), jnp.asarray(ones, BF16)


def _rw_params(lp):
    row = lambda a: a.reshape(1, -1)
    z = jnp.zeros((RW_LORA, RW_WIDTH), F32)
    w2a = jnp.concatenate([jnp.concatenate([lp["rw_w2"], z], axis=1),
                           jnp.concatenate([z, lp["rw_a2"]], axis=1)], axis=0).astype(BF16)
    return dict(mu=row(lp["shift_mu"]), w0=row(lp["rw_w0"]), a0=row(lp["rw_a0"]), w2a=w2a,
                k_k=row(lp["rw_k_k"]), k_a=row(lp["rw_k_a"]), r_k=row(lp["rw_r_k"]),
                ln_g=row(lp["rw_ln_g"]), ln_b=row(lp["rw_ln_b"]))


def _rwkv_prompt(proj, prev_row, s0_bd, rwp, tb):
    t = proj.shape[0]
    gsum, lincl, ones = _rw_consts()
    n = RW_GROUP * RW_CHUNK
    const = lambda shape: pl.BlockSpec(shape, lambda i: (0,) * len(shape))
    vec = const((1, RW_WIDTH))
    return pl.pallas_call(
        functools.partial(_rwkv_prompt_kernel, tb=tb),
        grid=(t // tb,),
        in_specs=[pl.BlockSpec((tb, RW_SHIFT_WIDTH), lambda i: (i, 0)),
                  const((1, RW_SHIFT_WIDTH)), const((RW_NGROUP, n, n)), const((1, RW_SHIFT_WIDTH)),
                  vec, vec, const((2 * RW_LORA, 2 * RW_WIDTH)), vec, vec, vec, vec, vec,
                  const((RW_WIDTH, RW_WIDTH)), const((RW_CHUNK, RW_CHUNK)), const((RW_CHUNK, 128))],
        out_specs=[pl.BlockSpec((tb, RW_WIDTH), lambda i: (i, 0)), const((RW_NGROUP, n, n))],
        out_shape=[jax.ShapeDtypeStruct((t, RW_WIDTH), F32), jax.ShapeDtypeStruct((RW_NGROUP, n, n), F32)],
        scratch_shapes=[pltpu.VMEM((RW_NGROUP, n, n), F32), pltpu.VMEM((1, RW_SHIFT_WIDTH), F32)]
        + [pltpu.VMEM((tb, RW_WIDTH), F32)] * 7,
        compiler_params=_cparams(1),
        name="rwkv_prompt",
    )(proj, prev_row, s0_bd, rwp["mu"], rwp["w0"], rwp["a0"], rwp["w2a"], rwp["k_k"], rwp["k_a"], rwp["r_k"],
      rwp["ln_g"], rwp["ln_b"], gsum, lincl, ones)


def _state_to_bd(s):
    st = jnp.swapaxes(s, -1, -2).reshape(RW_NGROUP, RW_GROUP, RW_HEAD_DIM, RW_HEAD_DIM)
    eye = jnp.eye(RW_GROUP, dtype=s.dtype)
    bd = st[:, :, :, None, :] * eye[None, :, None, :, None]
    n = RW_GROUP * RW_HEAD_DIM
    return bd.reshape(RW_NGROUP, n, n)


def _bd_to_state(bd):
    n = RW_GROUP * RW_HEAD_DIM
    b5 = bd.reshape(RW_NGROUP, RW_GROUP, RW_HEAD_DIM, RW_GROUP, RW_HEAD_DIM)
    idx = jnp.arange(RW_GROUP)
    blocks = b5[:, idx, :, idx, :]
    blocks = jnp.swapaxes(blocks, 0, 1).reshape(RW_HEADS, RW_HEAD_DIM, RW_HEAD_DIM)
    return jnp.swapaxes(blocks, -1, -2)


def _rwkv_sample_prep_kernel(p_ref, prev_ref, mu_ref, w0_ref, a0_ref, w2a_ref, kk_ref, ka_ref, rk_ref, gsum_ref,
                             r_ref, w_ref, k_ref, v_ref, kko_ref, b_ref, bonus_ref):
    r, lw, k, v, kk, b, bonus = _rw_prep(p_ref[...], prev_ref[...], mu_ref[...], w0_ref[...], a0_ref[...],
                                         w2a_ref[...], kk_ref[...], ka_ref[...], rk_ref[...], gsum_ref[...])
    r_ref[...] = r
    w_ref[...] = jnp.exp(lw)
    k_ref[...] = k
    v_ref[...] = v
    kko_ref[...] = kk
    b_ref[...] = b
    bonus_ref[...] = bonus


def _rwkv_sample_state_kernel(s_ref, r_ref, w_ref, k_ref, kk_ref, b_ref, v_ref, bonus_ref, lng_ref, lnb_ref,
                              y_ref, so_ref):
    s = s_ref[...]
    sa = jnp.sum(s * kk_ref[...], axis=-1, keepdims=True)
    s = s * w_ref[...] - sa * b_ref[...] + v_ref[...] * k_ref[...]
    so_ref[...] = s
    y = jnp.sum(s * r_ref[...], axis=-1, keepdims=True)
    mean = jnp.mean(y, axis=-2, keepdims=True)
    d = y - mean
    var = jnp.mean(d * d, axis=-2, keepdims=True)
    y_ref[...] = d * lax.rsqrt(var + GN_EPS) * lng_ref[...] + lnb_ref[...] + bonus_ref[...]


def _rwkv_sample(proj, prev_rows, state, rwp):
    b = proj.shape[0]
    gsum, _, _ = _rw_consts()
    full = lambda shape: pl.BlockSpec(shape, lambda i: (0,) * len(shape))
    vec = full((1, RW_WIDTH))
    outs = pl.pallas_call(
        _rwkv_sample_prep_kernel,
        grid=(1,),
        in_specs=[pl.BlockSpec((b, RW_SHIFT_WIDTH), lambda i: (0, 0)),
                  full((b, RW_SHIFT_WIDTH)), full((1, RW_SHIFT_WIDTH)), vec, vec,
                  full((2 * RW_LORA, 2 * RW_WIDTH)), vec, vec, vec, full((RW_WIDTH, RW_WIDTH))],
        out_specs=[full((b, RW_WIDTH))] * 7,
        out_shape=[jax.ShapeDtypeStruct((b, RW_WIDTH), F32)] * 7,
        compiler_params=_cparams(1),
        name="rwkv_sample_prep",
    )(proj, prev_rows, rwp["mu"], rwp["w0"], rwp["a0"], rwp["w2a"], rwp["k_k"], rwp["k_a"], rwp["r_k"], gsum)
    r, w, k, v, kk, bb, bonus = outs
    as_row = lambda a: a.reshape(b, RW_HEADS, 1, RW_HEAD_DIM)
    as_col = lambda a: a.reshape(-1, RW_HEADS, RW_HEAD_DIM, 1)
    nb = SAMPLE_ROWS_PER_STEP
    row_spec = pl.BlockSpec((nb, RW_HEADS, 1, RW_HEAD_DIM), lambda i: (i, 0, 0, 0))
    col_spec = pl.BlockSpec((nb, RW_HEADS, RW_HEAD_DIM, 1), lambda i: (i, 0, 0, 0))
    par_spec = pl.BlockSpec((RW_HEADS, RW_HEAD_DIM, 1), lambda i: (0, 0, 0))
    s_spec = pl.BlockSpec((nb, RW_HEADS, RW_HEAD_DIM, RW_HEAD_DIM), lambda i: (i, 0, 0, 0))
    y, s_new = pl.pallas_call(
        _rwkv_sample_state_kernel,
        grid=(b // nb,),
        in_specs=[s_spec, row_spec, row_spec, row_spec, row_spec, row_spec, col_spec, col_spec, par_spec, par_spec],
        out_specs=[col_spec, s_spec],
        out_shape=[jax.ShapeDtypeStruct((b, RW_HEADS, RW_HEAD_DIM, 1), F32),
                   jax.ShapeDtypeStruct(state.shape, F32)],
        compiler_params=_cparams(1),
        name="rwkv_sample_state",
    )(state, as_row(r), as_row(w), as_row(k), as_row(kk), as_row(bb), as_col(v), as_col(bonus),
      rwp["ln_g"].reshape(RW_HEADS, RW_HEAD_DIM, 1), rwp["ln_b"].reshape(RW_HEADS, RW_HEAD_DIM, 1))
    return y.reshape(b, RW_WIDTH), s_new


def _mem_prompt_kernel(q_ref, mk_ref, mv_ref, o_ref):
    scale = MEM_HEAD_DIM ** -0.5
    for h in range(MEM_HEADS):
        sl = slice(h * MEM_HEAD_DIM, (h + 1) * MEM_HEAD_DIM)
        s = _dot_nt(q_ref[:, sl].astype(BF16), mk_ref[:, sl].astype(BF16)) * scale
        p = jnp.exp(s - jnp.max(s, axis=1, keepdims=True))
        l = jnp.sum(p, axis=1, keepdims=True)
        o_ref[:, sl] = _dot(p.astype(BF16), mv_ref[:, sl].astype(BF16)) / l


def _mem_prompt(proj, mk, mv, tm):
    t = proj.shape[0]
    m = mk.shape[0]
    return pl.pallas_call(
        _mem_prompt_kernel,
        grid=(t // tm,),
        in_specs=[pl.BlockSpec((tm, MEM_WIDTH), lambda i: (i, COL_MQ // MEM_WIDTH)),
                  pl.BlockSpec((m, MEM_WIDTH), lambda i: (0, 0)),
                  pl.BlockSpec((m, MEM_WIDTH), lambda i: (0, 0))],
        out_specs=pl.BlockSpec((tm, MEM_WIDTH), lambda i: (i, 0)),
        out_shape=jax.ShapeDtypeStruct((t, MEM_WIDTH), F32),
        compiler_params=_cparams(1),
        name="mem_prompt",
    )(proj, mk, mv)


def _mem_sample_kernel(q_ref, mk_ref, mv_ref, o_ref):
    scale = MEM_HEAD_DIM ** -0.5
    rowh = lax.broadcasted_iota(jnp.int32, (8, MEM_WIDTH), 0)
    laneh = lax.broadcasted_iota(jnp.int32, (8, MEM_WIDTH), 1) // MEM_HEAD_DIM
    own = rowh == laneh
    for j in range(q_ref.shape[0]):
        q = q_ref[j]
        qblk = jnp.where(own, q, 0.0).astype(BF16)
        s = _dot_nt(qblk, mk_ref[j].astype(BF16)) * scale
        p = jnp.exp(s - jnp.max(s, axis=1, keepdims=True))
        l = jnp.sum(p, axis=1, keepdims=True)
        o = _dot(p.astype(BF16), mv_ref[j].astype(BF16)) / l
        o_ref[j] = jnp.sum(jnp.where(own, o, 0.0), axis=0, keepdims=True)


def _mem_sample(q, mk, mv):
    b, m, _ = mk.shape
    nb = SAMPLE_ROWS_PER_STEP
    q_spec = pl.BlockSpec((nb, 1, MEM_WIDTH), lambda i: (i, 0, 0))
    kv_spec = pl.BlockSpec((nb, m, MEM_WIDTH), lambda i: (i, 0, 0))
    out = pl.pallas_call(
        _mem_sample_kernel,
        grid=(b // nb,),
        in_specs=[q_spec, kv_spec, kv_spec],
        out_specs=q_spec,
        out_shape=jax.ShapeDtypeStruct((b, 1, MEM_WIDTH), F32),
        compiler_params=_cparams(1),
        name="mem_sample",
    )(q.reshape(b, 1, MEM_WIDTH), mk, mv)
    return out.reshape(b, MEM_WIDTH)


def _merge_kernel(x_ref, gl_ref, osb_ref, sz_ref, orw_ref, rz_ref, omem_ref, mz_ref,
                  wsb_ref, wrw_ref, wmem_ref, wo_ref, fg_ref, o_ref, *, final):
    a_sb = (osb_ref[...] * _silu(sz_ref[...])).astype(BF16)
    a_rw = (orw_ref[...] * _silu(rz_ref[...])).astype(BF16)
    a_mem = (omem_ref[...] * _silu(mz_ref[...])).astype(BF16)
    merged = (_sigmoid(gl_ref[:, 0:D_MODEL]) * _dot(a_sb, wsb_ref[...])
              + _sigmoid(gl_ref[:, D_MODEL:2 * D_MODEL]) * _dot(a_rw, wrw_ref[...])
              + _sigmoid(gl_ref[:, 2 * D_MODEL:3 * D_MODEL]) * _dot(a_mem, wmem_ref[...]))
    y = x_ref[...] + _dot(merged.astype(BF16), wo_ref[...])
    if final:
        ms = jnp.mean(y * y, axis=-1, keepdims=True)
        y = y * lax.rsqrt(ms + RMS_EPS) * fg_ref[...]
    o_ref[...] = y


def _merge(x, proj, o_sb, o_rw, o_mem, w_sb, w_rw, w_mem, w_o, final_g, final, tm, name):
    t = x.shape[0]
    col = lambda c: pl.BlockSpec((tm, SB_WIDTH), lambda i: (i, c // SB_WIDTH))
    act = pl.BlockSpec((tm, SB_WIDTH), lambda i: (i, 0))
    wspec = pl.BlockSpec((SB_WIDTH, D_MODEL), lambda i: (0, 0))
    return pl.pallas_call(
        functools.partial(_merge_kernel, final=final),
        grid=(t // tm,),
        in_specs=[pl.BlockSpec((tm, D_MODEL), lambda i: (i, 0)),
                  pl.BlockSpec((tm, 3 * D_MODEL), lambda i: (i, 0)),
                  act, col(COL_SZ), act, col(COL_RZ), act, col(COL_MZ),
                  wspec, wspec, wspec, pl.BlockSpec((D_MODEL, D_MODEL), lambda i: (0, 0)),
                  pl.BlockSpec((1, D_MODEL), lambda i: (0, 0))],
        out_specs=pl.BlockSpec((tm, D_MODEL), lambda i: (i, 0)),
        out_shape=jax.ShapeDtypeStruct((t, D_MODEL), F32),
        compiler_params=_cparams(1),
        name=name,
    )(x, proj, o_sb, proj, o_rw, proj, o_mem, proj, w_sb, w_rw, w_mem, w_o, final_g.reshape(1, D_MODEL))


def _split_w_in(w_in):
    sq, sk, sv, sz, rp, rz, mq, mz, gl = jnp.split(
        w_in, np.cumsum([SB_WIDTH] * 4 + [RW_SHIFT_WIDTH, RW_WIDTH, MEM_WIDTH, MEM_WIDTH]).tolist(), axis=1)
    w_act = jnp.concatenate([gl, sq, sz, rz, mq, mz], axis=1).astype(BF16)
    return w_act, rp.astype(BF16), sk.T.astype(BF16), sv.T.astype(BF16)


def _pages_from_t(xt, page):
    t = xt.shape[1]
    x = xt.reshape(SB_HEADS, SB_HEAD_DIM, t // page, page)
    return jnp.transpose(x, (2, 3, 0, 1))[None]


def kernel(x_prompt, x_sample, cache_sb_k, cache_sb_v, cache_mem_k, cache_mem_v, state_wkv, state_shift, page_table, mem_prompt, norm_g, w_in, sb_bias, shift_mu, rw_w0, rw_w2, rw_a0, rw_a2, rw_k_k, rw_k_a, rw_r_k, rw_ln_g, rw_ln_b, mem_norm_g, w_mem_kv, w_bo_sb, w_bo_rw, w_bo_mem, w_o, final_norm_g):
    depth = w_in.shape[0]
    bp, tp, _ = x_prompt.shape
    bs, ts, _ = x_sample.shape
    assert bp == 1 and ts == 1
    page = cache_sb_k.shape[2]
    n_mem = mem_prompt.shape[1]
    sb_scale = SB_HEAD_DIM ** -0.5

    hp = x_prompt.reshape(tp, D_MODEL)
    hs = x_sample.reshape(bs, D_MODEL)
    mem = mem_prompt.reshape(n_mem, D_MODEL)
    k_pool_t = jnp.transpose(cache_sb_k, (0, 1, 3, 4, 2))
    v_pool_t = jnp.transpose(cache_sb_v, (0, 1, 3, 4, 2))
    outs = {k: [] for k in ("sbk_p", "sbv_p", "mk_p", "mv_p", "wkv_p", "shift_p", "sbk_s", "sbv_s", "wkv_s", "shift_s")}
    for l in range(depth):
        last = l == depth - 1
        lp = dict(shift_mu=shift_mu[l], rw_w0=rw_w0[l], rw_w2=rw_w2[l], rw_a0=rw_a0[l], rw_a2=rw_a2[l],
                  rw_k_k=rw_k_k[l], rw_k_a=rw_k_a[l], rw_r_k=rw_r_k[l], rw_ln_g=rw_ln_g[l], rw_ln_b=rw_ln_b[l])
        rwp = _rw_params(lp)
        w_act, w_rp, w_kt, w_vt = _split_w_in(w_in[l])
        w_sb, w_rw, w_mem, w_out = (w.astype(BF16) for w in (w_bo_sb[l], w_bo_rw[l], w_bo_mem[l], w_o[l]))

        act, rp, kt, vt = _in_proj(hp, norm_g[l], w_act, w_rp, w_kt, w_vt, 512, "proj_prompt")
        kv = _norm_proj(mem, mem_norm_g[l], w_mem_kv[l].astype(BF16), n_mem, MEM_WIDTH, "mem_kv")
        mk, mv = kv[:, :MEM_WIDTH], kv[:, MEM_WIDTH:]
        o_sb = _sb_prompt(act, kt.reshape(SB_HEADS, SB_HEAD_DIM, tp), vt.reshape(SB_HEADS, SB_HEAD_DIM, tp),
                          sb_bias[l], 512, 1024)
        s0 = _state_to_bd(jnp.zeros((RW_HEADS, RW_HEAD_DIM, RW_HEAD_DIM), F32))
        o_rw, s_bd = _rwkv_prompt(rp, jnp.zeros((1, RW_SHIFT_WIDTH), F32), s0, rwp, 512)
        o_mem = _mem_prompt(act, mk, mv, 512)
        hp = _merge(hp, act, o_sb, o_rw, o_mem, w_sb, w_rw, w_mem, w_out, final_norm_g, last, 256, "merge_prompt")
        outs["sbk_p"].append(_pages_from_t(kt, page))
        outs["sbv_p"].append(_pages_from_t(vt, page))
        outs["mk_p"].append(mk.reshape(bp, n_mem, MEM_HEADS, MEM_HEAD_DIM))
        outs["mv_p"].append(mv.reshape(bp, n_mem, MEM_HEADS, MEM_HEAD_DIM))
        outs["wkv_p"].append(_bd_to_state(s_bd).reshape(bp, RW_HEADS, RW_HEAD_DIM, RW_HEAD_DIM))
        outs["shift_p"].append(rp[tp - 1:tp])

        act_s, rp_s, kt_s, vt_s = _in_proj(hs, norm_g[l], w_act, w_rp, w_kt, w_vt, bs, "proj_sample")
        o_sb_s = _sb_sample(act_s[:, COL_SQ:COL_SQ + SB_WIDTH] * sb_scale, sb_bias[l], k_pool_t, v_pool_t, l,
                            page_table, 16)
        o_rw_s, s_new = _rwkv_sample(rp_s, state_shift[l], state_wkv[l], rwp)
        o_mem_s = _mem_sample(act_s[:, COL_MQ:COL_MQ + MEM_WIDTH],
                              cache_mem_k[l].reshape(bs, n_mem, MEM_WIDTH),
                              cache_mem_v[l].reshape(bs, n_mem, MEM_WIDTH))
        hs = _merge(hs, act_s, o_sb_s, o_rw_s, o_mem_s, w_sb, w_rw, w_mem, w_out, final_norm_g, last, bs,
                    "merge_sample")
        outs["sbk_s"].append(kt_s.T.reshape(bs, ts, SB_HEADS, SB_HEAD_DIM))
        outs["sbv_s"].append(vt_s.T.reshape(bs, ts, SB_HEADS, SB_HEAD_DIM))
        outs["wkv_s"].append(s_new)
        outs["shift_s"].append(rp_s)

    st = {k: jnp.stack(v) for k, v in outs.items()}
    return (hp.reshape(bp, tp, D_MODEL), hs.reshape(bs, ts, D_MODEL),
            st["sbk_p"], st["sbv_p"], st["mk_p"], st["mv_p"], st["wkv_p"], st["shift_p"],
            st["sbk_s"], st["sbv_s"], st["wkv_s"], st["shift_s"])
```

```python
import functools

import numpy as np
import jax
import jax.numpy as jnp
from jax import lax
from jax.experimental import pallas as pl
from jax.experimental.pallas import tpu as pltpu

F32 = jnp.float32
BF16 = jnp.bfloat16

D_MODEL = 1024
SB_HEADS = 8
SB_HEAD_DIM = 64
SB_WIDTH = SB_HEADS * SB_HEAD_DIM
RW_HEADS = 8
RW_HEAD_DIM = 64
RW_WIDTH = RW_HEADS * RW_HEAD_DIM
RW_LORA = 64
RW_SHIFT_WIDTH = 3 * RW_WIDTH + 2 * RW_LORA
MEM_HEADS = 4
MEM_HEAD_DIM = 128
MEM_WIDTH = MEM_HEADS * MEM_HEAD_DIM
GN_EPS = 64e-5
RMS_EPS = 1e-6
C_IN = 4 * SB_WIDTH + RW_SHIFT_WIDTH + RW_WIDTH + 2 * MEM_WIDTH + 3 * D_MODEL

COL_GATE = 0
COL_SQ = 3 * D_MODEL
COL_SZ = COL_SQ + SB_WIDTH
COL_RZ = COL_SZ + SB_WIDTH
COL_MQ = COL_RZ + RW_WIDTH
COL_MZ = COL_MQ + MEM_WIDTH
ACT_WIDTH = COL_MZ + MEM_WIDTH
ACT_TILES = 2
assert ACT_WIDTH + RW_SHIFT_WIDTH + 2 * SB_WIDTH == C_IN and ACT_WIDTH % (ACT_TILES * 128) == 0
LOG2E = 1.4426950408889634
SB_KPAD = 128

VMEM_LIMIT_BYTES = 56 * 1024 * 1024
MXU_TILE = 256

RW_CHUNK = 64
RW_GROUP = MXU_TILE // RW_CHUNK
RW_NGROUP = RW_HEADS // RW_GROUP
RW_LOCKSTEP = 4
SB_SUB = MXU_TILE
SAMPLE_ROWS_PER_STEP = 8


def _cparams(n_grid):
    return pltpu.CompilerParams(dimension_semantics=("arbitrary",) * n_grid,
                                vmem_limit_bytes=VMEM_LIMIT_BYTES)


def _sigmoid(x):
    return 1.0 / (1.0 + jnp.exp(-x))


def _softplus(x):
    return jnp.maximum(x, 0.0) + jnp.log(1.0 + jnp.exp(-jnp.abs(x)))


def _silu(x):
    return x * _sigmoid(x)


def _split_bf16(x):
    hi = x.astype(BF16)
    lo = (x - hi.astype(F32)).astype(BF16)
    return hi, lo


def _dot(a, b):
    return jnp.dot(a, b, preferred_element_type=F32)


def _dot_nt(a, b):
    return lax.dot_general(a, b, (((1,), (1,)), ((), ())), preferred_element_type=F32)


def _dot_tn(a, b):
    return lax.dot_general(a, b, (((0,), (0,)), ((), ())), preferred_element_type=F32)


def _dot_hilo(x, w01):
    hi, lo = _split_bf16(x)
    return _dot(hi, w01) + _dot(lo, w01)


def _norm_proj_kernel(x_ref, g_ref, w_ref, o_ref, h_ref):
    @pl.when(pl.program_id(1) == 0)
    def _():
        x = x_ref[...]
        ms = jnp.mean(x * x, axis=-1, keepdims=True)
        h_ref[...] = (x * lax.rsqrt(ms + RMS_EPS) * g_ref[...]).astype(BF16)

    o_ref[...] = _dot(h_ref[...], w_ref[...])


def _norm_proj(x, g, w, tm, tn, name):
    t, d = x.shape
    n = w.shape[1]
    return pl.pallas_call(
        _norm_proj_kernel,
        grid=(t // tm, n // tn),
        in_specs=[pl.BlockSpec((tm, d), lambda i, j: (i, 0)),
                  pl.BlockSpec((1, d), lambda i, j: (0, 0)),
                  pl.BlockSpec((d, tn), lambda i, j: (0, j))],
        out_specs=pl.BlockSpec((tm, tn), lambda i, j: (i, j)),
        out_shape=jax.ShapeDtypeStruct((t, n), F32),
        scratch_shapes=[pltpu.VMEM((tm, d), BF16)],
        compiler_params=_cparams(2),
        name=name,
    )(x, g.reshape(1, d), w)


def _in_proj_kernel(x_ref, g_ref, *refs):
    wa_refs = refs[:ACT_TILES]
    wrp_ref, wkt_ref, wvt_ref, act_ref, rp_ref, kt_ref, vt_ref, h_ref = refs[ACT_TILES:]
    j = pl.program_id(1)

    @pl.when(j == 0)
    def _():
        x = x_ref[...]
        ms = jnp.mean(x * x, axis=-1, keepdims=True)
        h_ref[...] = (x * lax.rsqrt(ms + RMS_EPS) * g_ref[...]).astype(BF16)

    for tile, wa_ref in enumerate(wa_refs):
        @pl.when(j == tile)
        def _():
            act_ref[...] = _dot(h_ref[...], wa_ref[...])

    @pl.when(j == ACT_TILES)
    def _():
        rp_ref[...] = _dot(h_ref[...], wrp_ref[...])

    @pl.when(j == ACT_TILES + 1)
    def _():
        h = h_ref[...]
        kt_ref[...] = _dot_nt(wkt_ref[...], h)
        vt_ref[...] = _dot_nt(wvt_ref[...], h)


def _in_proj(x, g, w_act, w_rp, w_kt, w_vt, tm, name):
    t, d = x.shape
    tn = ACT_WIDTH // ACT_TILES
    last_act = ACT_TILES - 1
    const = lambda shape: pl.BlockSpec(shape, lambda i, j: (0, 0), pipeline_mode=pl.Buffered(1))
    return pl.pallas_call(
        _in_proj_kernel,
        grid=(t // tm, ACT_TILES + 2),
        in_specs=[pl.BlockSpec((tm, d), lambda i, j: (i, 0)), const((1, d))]
        + [pl.BlockSpec((d, tn), lambda i, j, tile=tile: (0, tile), pipeline_mode=pl.Buffered(1))
           for tile in range(ACT_TILES)]
        + [const((d, RW_SHIFT_WIDTH)), const((SB_WIDTH, d)), const((SB_WIDTH, d))],
        out_specs=[pl.BlockSpec((tm, tn), lambda i, j: (i, jnp.minimum(j, last_act))),
                   pl.BlockSpec((tm, RW_SHIFT_WIDTH), lambda i, j: (i, 0)),
                   pl.BlockSpec((SB_WIDTH, tm), lambda i, j: (0, i)),
                   pl.BlockSpec((SB_WIDTH, tm), lambda i, j: (0, i))],
        out_shape=[jax.ShapeDtypeStruct((t, ACT_WIDTH), F32), jax.ShapeDtypeStruct((t, RW_SHIFT_WIDTH), F32),
                   jax.ShapeDtypeStruct((SB_WIDTH, t), F32), jax.ShapeDtypeStruct((SB_WIDTH, t), F32)],
        scratch_shapes=[pltpu.VMEM((tm, d), BF16)],
        compiler_params=_cparams(2),
        name=name,
    )(x, g.reshape(1, d), *([w_act] * ACT_TILES), w_rp, w_kt, w_vt)


def _sb_prompt_kernel(qi_ref, kj_ref, bias_ref, q_ref, kt_ref, vt_ref, tri_ref, o_ref, q_scr, acc_ref, carry_ref,
                      z_scr, *, bq, bk):
    s = pl.program_id(0)
    qi = qi_ref[s]
    kj = kj_ref[s]
    diag = kj == (qi * bq) // bk
    pad = SB_KPAD - SB_HEAD_DIM

    @pl.when(diag)
    def _():
        acc_ref[...] = jnp.zeros_like(acc_ref)
        carry_ref[...] = jnp.zeros_like(carry_ref)
        lane = lax.broadcasted_iota(jnp.int32, (bq, pad), 1)
        for h in range(SB_HEADS):
            qh = q_ref[:, h * SB_HEAD_DIM:(h + 1) * SB_HEAD_DIM] * (SB_HEAD_DIM ** -0.5 * LOG2E)
            ext = jnp.where(lane == 0, bias_ref[0, h],
                            jnp.where(lane == 1, bias_ref[1, h], jnp.where(lane == 2, bias_ref[2, h], 0.0)))
            q_scr[h] = jnp.concatenate([qh, ext], axis=1).astype(BF16)

    ones_rows = (lax.broadcasted_iota(jnp.int32, (pad, SB_SUB), 0) < 3).astype(BF16)

    def logits(h, kb):
        ks = slice(kb * SB_SUB, (kb + 1) * SB_SUB)
        return _dot(q_scr[h], jnp.concatenate([kt_ref[h, :, ks].astype(BF16), ones_rows], axis=0))

    def run(row_off):
        kbs = []
        for kb in reversed(range(bk // SB_SUB)):
            if row_off is None or (kb + 1) * SB_SUB <= row_off:
                kbs.append((kb, False))
            elif kb * SB_SUB < row_off + bq:
                kbs.append((kb, True))
        n_sub = len(kbs)
        assert n_sub % 2 == 0
        z_scr[0] = logits(0, kbs[0][0])

        def head(h, _):
            tri = tri_ref[...]
            c = carry_ref[h]
            acc = acc_ref[h]
            for i, (kb, masked) in enumerate(kbs):
                ks = slice(kb * SB_SUB, (kb + 1) * SB_SUB)
                z = z_scr[i % 2]
                z_scr[(i + 1) % 2] = (logits(h, kbs[i + 1][0]) if i + 1 < n_sub
                                      else logits((h + 1) % SB_HEADS, kbs[0][0]))
                neg_abs = pltpu.bitcast(pltpu.bitcast(z, jnp.uint32) | jnp.uint32(0x80000000), F32)
                sp = jnp.maximum(z, 0.0) + jnp.log(1.0 + jnp.exp2(neg_abs)) * LOG2E
                if masked:
                    row = lax.broadcasted_iota(jnp.int32, (bq, SB_SUB), 0) + row_off
                    col = lax.broadcasted_iota(jnp.int32, (bq, SB_SUB), 1) + kb * SB_SUB
                    vis = col < row
                    sp = jnp.where(vis, sp, 0.0)
                within = _dot(sp.astype(BF16), tri)
                w = jnp.exp2(z - sp - within - c)
                if masked:
                    w = jnp.where(vis, w, 0.0)
                acc = acc + _dot_nt(w.astype(BF16), vt_ref[h, :, ks].astype(BF16))
                c = c + (within[:, :1] + sp[:, :1])
            carry_ref[h] = c
            acc_ref[h] = acc
            return 0

        lax.fori_loop(0, SB_HEADS, head, 0)

    for row_off in range(0, bk, bq):
        @pl.when(jnp.logical_and(diag, (qi * bq) % bk == row_off))
        def _():
            run(row_off)

    @pl.when(jnp.logical_not(diag))
    def _():
        run(None)

    @pl.when(kj == 0)
    def _():
        for h in range(SB_HEADS):
            o_ref[:, h * SB_HEAD_DIM:(h + 1) * SB_HEAD_DIM] = acc_ref[h]


def _sb_prompt(act, kt, vt, bias, bq, bk):
    t = act.shape[0]
    h, d = SB_HEADS, SB_HEAD_DIM
    nq = t // bq
    assert bk % bq == 0 and t % bk == 0
    first = [(i * bq) // bk for i in range(nq)]
    qi = np.concatenate([np.full(first[i] + 1, i) for i in range(nq)]).astype(np.int32)
    kj = np.concatenate([np.arange(first[i], -1, -1) for i in range(nq)]).astype(np.int32)
    tri = (np.arange(SB_SUB)[:, None] > np.arange(SB_SUB)[None, :]).astype(np.float32)
    b2 = bias.astype(F32) * LOG2E
    b_hi = b2.astype(BF16).astype(F32)
    b_mid = (b2 - b_hi).astype(BF16).astype(F32)
    b_lo = (b2 - b_hi - b_mid).astype(BF16).astype(F32)
    grid_spec = pltpu.PrefetchScalarGridSpec(
        num_scalar_prefetch=2,
        grid=(len(qi),),
        in_specs=[pl.BlockSpec(memory_space=pltpu.SMEM),
                  pl.BlockSpec((bq, SB_WIDTH), lambda s, qi, kj: (qi[s], COL_SQ // SB_WIDTH)),
                  pl.BlockSpec((h, d, bk), lambda s, qi, kj: (0, 0, kj[s])),
                  pl.BlockSpec((h, d, bk), lambda s, qi, kj: (0, 0, kj[s])),
                  pl.BlockSpec((SB_SUB, SB_SUB), lambda s, qi, kj: (0, 0))],
        out_specs=pl.BlockSpec((bq, SB_WIDTH), lambda s, qi, kj: (qi[s], 0)),
        scratch_shapes=[pltpu.VMEM((h, bq, SB_KPAD), BF16), pltpu.VMEM((h, bq, d), F32),
                        pltpu.VMEM((h, bq, 1), F32), pltpu.VMEM((2, bq, SB_SUB), F32)],
    )
    return pl.pallas_call(
        functools.partial(_sb_prompt_kernel, bq=bq, bk=bk),
        grid_spec=grid_spec,
        out_shape=jax.ShapeDtypeStruct((t, SB_WIDTH), F32),
        compiler_params=_cparams(1),
        name="sb_prompt",
    )(jnp.asarray(qi), jnp.asarray(kj), jnp.stack([b_hi, b_mid, b_lo]), act, kt, vt, jnp.asarray(tri, BF16))


def _sb_sample_kernel(pt_ref, q_ref, bias_ref, *refs, pp, n_steps):
    k_refs = refs[:pp]
    v_refs = refs[pp:2 * pp]
    tri_ref, o_ref, acc_ref, carry_ref, z_scr, w_scr = refs[2 * pp:]
    g = pl.program_id(1)

    @pl.when(g == 0)
    def _():
        acc_ref[...] = jnp.zeros_like(acc_ref)
        carry_ref[...] = jnp.zeros_like(carry_ref)

    q = q_ref[0]
    for i in range(pp):
        for h in range(SB_HEADS):
            z_scr[i * SB_HEADS + h:i * SB_HEADS + h + 1, :] = jnp.sum(k_refs[i][0, 0, h] * q[h], axis=0, keepdims=True)
    z = z_scr[...] + jnp.concatenate([bias_ref[...]] * pp, axis=0)
    sp = _softplus(z)
    within = _dot_hilo(sp, tri_ref[...])
    tot = jnp.sum(sp, axis=1, keepdims=True)
    c = carry_ref[...]
    cs = [None] * pp
    for i in reversed(range(pp)):
        cs[i] = c
        c = c + tot[i * SB_HEADS:(i + 1) * SB_HEADS]
    carry_ref[...] = c
    w_scr[...] = jnp.exp(z - sp - within - jnp.concatenate(cs, axis=0))
    for h in range(SB_HEADS):
        a = acc_ref[h]
        for i in range(pp):
            a = a + v_refs[i][0, 0, h] * w_scr[i * SB_HEADS + h:i * SB_HEADS + h + 1, :]
        acc_ref[h] = a

    @pl.when(g == n_steps - 1)
    def _():
        o_ref[0] = jnp.sum(acc_ref[...], axis=2, keepdims=True)


def _sb_sample(q, bias, k_pool, v_pool, layer, page_table, pp):
    b = q.shape[0]
    n_pages = page_table.shape[1]
    page = k_pool.shape[-1]
    n_steps = n_pages // pp
    tri = (np.arange(page)[:, None] > np.arange(page)[None, :]).astype(np.float32)

    def page_spec(i):
        return pl.BlockSpec((1, 1, SB_HEADS, SB_HEAD_DIM, page),
                            lambda bi, g, pt: (layer, pt[bi, (n_steps - 1 - g) * pp + i], 0, 0, 0))

    q_spec = pl.BlockSpec((1, SB_HEADS, SB_HEAD_DIM, 1), lambda bi, g, pt: (bi, 0, 0, 0))
    grid_spec = pltpu.PrefetchScalarGridSpec(
        num_scalar_prefetch=1,
        grid=(b, n_steps),
        in_specs=[q_spec, pl.BlockSpec((SB_HEADS, 1), lambda bi, g, pt: (0, 0))]
        + [page_spec(i) for i in range(pp)] + [page_spec(i) for i in range(pp)]
        + [pl.BlockSpec((page, page), lambda bi, g, pt: (0, 0))],
        out_specs=q_spec,
        scratch_shapes=[pltpu.VMEM((SB_HEADS, SB_HEAD_DIM, page), F32), pltpu.VMEM((SB_HEADS, 1), F32),
                        pltpu.VMEM((SB_HEADS * pp, page), F32), pltpu.VMEM((SB_HEADS * pp, page), F32)],
    )
    out = pl.pallas_call(
        functools.partial(_sb_sample_kernel, pp=pp, n_steps=n_steps),
        grid_spec=grid_spec,
        out_shape=jax.ShapeDtypeStruct((b, SB_HEADS, SB_HEAD_DIM, 1), F32),
        compiler_params=_cparams(2),
        name="sb_sample",
    )(page_table, q.reshape(b, SB_HEADS, SB_HEAD_DIM, 1), bias.reshape(SB_HEADS, 1),
      *([k_pool] * pp), *([v_pool] * pp), jnp.asarray(tri, BF16))
    return out.reshape(b, SB_WIDTH)


def _rw_prep(p, prev, mu, w0, a0, w2a, k_k, k_a, r_k, gsum):
    u = p + mu * (prev - p)
    r = u[:, 0:RW_WIDTH]
    k = u[:, RW_WIDTH:2 * RW_WIDTH]
    v = u[:, 2 * RW_WIDTH:3 * RW_WIDTH]
    x2 = u[:, 3 * RW_WIDTH:]
    lane = lax.broadcasted_iota(jnp.int32, x2.shape, 1)
    x2 = jnp.where(lane < RW_LORA, jnp.tanh(x2), x2)
    d = _dot(x2.astype(BF16), w2a)
    w_log = -_softplus(-(w0 + d[:, :RW_WIDTH])) - 0.5
    lw = -jnp.exp(w_log)
    a = _sigmoid(a0 + d[:, RW_WIDTH:])
    kk = k * k_k
    kk = kk / jnp.maximum(jnp.sqrt(_dot_hilo(kk * kk, gsum)), 1e-12)
    k = k * (1.0 + (a - 1.0) * k_a)
    bonus = _dot_hilo(r * k * r_k, gsum) * v
    return r, lw, k, v, kk, kk * a, bonus


def _neumann_inverse(a_list, eye, bdiag):
    x = [eye - a for a in a_list]
    p = [_dot(a.astype(BF16), bdiag(a)) for a in a_list]
    order = 2
    while 2 * order < RW_CHUNK:
        pb = [bdiag(pi) for pi in p]
        x = [xi + _dot(xi.astype(BF16), pbi) for xi, pbi in zip(x, pb)]
        p = [_dot(pi.astype(BF16), pbi) for pi, pbi in zip(p, pb)]
        order *= 2
    return [xi + _dot(xi.astype(BF16), bdiag(pi)) for xi, pi in zip(x, p)]


def _rwkv_prompt_kernel(p_ref, prev_ref, s0_ref, mu_ref, w0_ref, a0_ref, w2a_ref, kk_ref, ka_ref, rk_ref,
                        lng_ref, lnb_ref, gsum_ref, lincl_ref, ones_ref,
                        y_ref, sout_ref,
                        s_scr, last_scr, r_scr, lw_scr, k_scr, v_scr, kk_scr, b_scr, y_scr, *, tb):
    i = pl.program_id(0)

    @pl.when(i == 0)
    def _():
        s_scr[...] = s0_ref[...]
        last_scr[...] = prev_ref[...]

    p = p_ref[...]
    row0 = lax.broadcasted_iota(jnp.int32, p.shape, 0) == 0
    prev = jnp.where(row0, last_scr[...], pltpu.roll(p, 1, axis=0))
    last_scr[...] = p[tb - 1:tb, :]
    gsum = gsum_ref[...]
    r, lw, k, v, kk, b, bonus = _rw_prep(p, prev, mu_ref[...], w0_ref[...], a0_ref[...], w2a_ref[...],
                                         kk_ref[...], ka_ref[...], rk_ref[...], gsum)
    r_scr[...] = r
    lw_scr[...] = lw
    k_scr[...] = k
    v_scr[...] = v
    kk_scr[...] = kk
    b_scr[...] = b
    y_ref[...] = bonus

    n = RW_GROUP * RW_CHUNK
    row = lax.broadcasted_iota(jnp.int32, (n, n), 0)
    col = lax.broadcasted_iota(jnp.int32, (n, n), 1)
    same_head = (row // RW_CHUNK) == (col // RW_HEAD_DIM)
    trow = lax.broadcasted_iota(jnp.int32, (RW_CHUNK, n), 0)
    tcol = lax.broadcasted_iota(jnp.int32, (RW_CHUNK, n), 1) % RW_CHUNK
    strict = tcol < trow
    incl = tcol <= trow
    eye = (tcol == trow).astype(F32)

    def bdiag(x):
        return jnp.where(same_head, jnp.concatenate([x] * RW_GROUP, axis=0), 0.0).astype(BF16)

    groups = range(RW_NGROUP)
    sls = [slice(g * n, (g + 1) * n) for g in groups]

    def decayed(ci):
        rows = pl.ds(pl.multiple_of(ci * RW_CHUNK, RW_CHUNK), RW_CHUNK)
        lwc = lw_scr[rows, :]
        lhi, llo = _split_bf16(lwc)
        lincl = lincl_ref[...]
        cum = _dot(lincl, lhi) + _dot(lincl, llo)
        cum_end = cum[RW_CHUNK - 1:RW_CHUNK, :]
        ones = ones_ref[...]
        g_end_col = jnp.exp(_dot_tn(lhi, ones) + _dot_tn(llo, ones))
        g_inv = jnp.exp(-cum)
        g_out = jnp.exp(cum_end - cum)
        rc = r_scr[rows, :] * jnp.exp(cum)
        kkc = kk_scr[rows, :] * jnp.exp(cum - lwc)
        bc = b_scr[rows, :]
        kc = k_scr[rows, :]
        return dict(rows=rows, g_end_col=g_end_col, rc=rc, kkc=kkc, vc=v_scr[rows, :],
                    bt=bc * g_inv, kt=kc * g_inv, bh=bc * g_out, kh=kc * g_out)

    def chunks(ci, _):
        cs = [decayed(ci * RW_LOCKSTEP + j) for j in range(RW_LOCKSTEP)]
        probs = [(c, sl) for c in cs for sl in sls]
        kr = [jnp.concatenate([c["kkc"][:, sl], c["rc"][:, sl]], axis=0).astype(BF16) for c, sl in probs]
        m_b = [_dot_nt(kr[i], bdiag(c["bt"][:, sl])) for i, (c, sl) in enumerate(probs)]
        m_k = [_dot_nt(kr[i], bdiag(c["kt"][:, sl])) for i, (c, sl) in enumerate(probs)]
        t_inv = _neumann_inverse([jnp.where(strict, m[:RW_CHUNK], 0.0) for m in m_b], eye, bdiag)
        ap_k = [jnp.concatenate([jnp.where(strict, m[:RW_CHUNK], 0.0), jnp.where(incl, m[RW_CHUNK:], 0.0)],
                                axis=0).astype(BF16) for m in m_k]
        from_v = [_dot(ap_k[i], bdiag(c["vc"][:, sl])) for i, (c, sl) in enumerate(probs)]
        p_b = [jnp.where(incl, m[RW_CHUNK:], 0.0).astype(BF16) for m in m_b]
        for j, c in enumerate(cs):
            ids = [j * RW_NGROUP + g for g in groups]
            s_old = [s_scr[g] for g in groups]
            from_s = [_dot(kr[i], s_old[g].astype(BF16)) for g, i in zip(groups, ids)]
            u = [-_dot(t_inv[i].astype(BF16), bdiag(from_s[g][:RW_CHUNK] + from_v[i][:RW_CHUNK]))
                 for g, i in zip(groups, ids)]
            for g, i in zip(groups, ids):
                c_y = from_s[g][RW_CHUNK:] + _dot(p_b[i], bdiag(u[g])) + from_v[i][RW_CHUNK:]
                y_scr[c["rows"], sls[g]] = c_y
            upd = [_dot_tn(jnp.concatenate([c["bh"][:, sls[g]], c["kh"][:, sls[g]]], axis=0).astype(BF16),
                           jnp.concatenate([u[g], c["vc"][:, sls[g]]], axis=0).astype(BF16)) for g in groups]
            for g in groups:
                g_end = jnp.concatenate([c["g_end_col"][sls[g], :]] * (n // 128), axis=1)
                s_scr[g] = g_end * s_old[g] + jnp.where(same_head, upd[g], 0.0)
        return 0

    lax.fori_loop(0, tb // (RW_CHUNK * RW_LOCKSTEP), chunks, 0)

    y = y_scr[...]
    inv_n = 1.0 / RW_HEAD_DIM
    mean = _dot_hilo(y, gsum) * inv_n
    d = y - mean
    var = _dot_hilo(d * d, gsum) * inv_n
    y_ref[...] = d * lax.rsqrt(var + GN_EPS) * lng_ref[...] + lnb_ref[...] + y_ref[...]

    @pl.when(i == pl.num_programs(0) - 1)
    def _():
        sout_ref[...] = s_scr[...]


def _rw_consts():
    hd = np.arange(RW_WIDTH) // RW_HEAD_DIM
    gsum = (hd[:, None] == hd[None, :]).astype(np.float32)
    lincl = (np.arange(RW_CHUNK)[:, None] >= np.arange(RW_CHUNK)[None, :]).astype(np.float32)
    ones = np.ones((RW_CHUNK, 128), np.float32)
    return jnp.asarray(gsum, BF16), jnp.asarray(lincl, BF16), jnp.asarray(ones, BF16)


def _rw_params(lp):
    row = lambda a: a.reshape(1, -1)
    z = jnp.zeros((RW_LORA, RW_WIDTH), F32)
    w2a = jnp.concatenate([jnp.concatenate([lp["rw_w2"], z], axis=1),
                           jnp.concatenate([z, lp["rw_a2"]], axis=1)], axis=0).astype(BF16)
    return dict(mu=row(lp["shift_mu"]), w0=row(lp["rw_w0"]), a0=row(lp["rw_a0"]), w2a=w2a,
                k_k=row(lp["rw_k_k"]), k_a=row(lp["rw_k_a"]), r_k=row(lp["rw_r_k"]),
                ln_g=row(lp["rw_ln_g"]), ln_b=row(lp["rw_ln_b"]))


def _rwkv_prompt(proj, prev_row, s0_bd, rwp, tb):
    t = proj.shape[0]
    gsum, lincl, ones = _rw_consts()
    n = RW_GROUP * RW_CHUNK
    const = lambda shape: pl.BlockSpec(shape, lambda i: (0,) * len(shape))
    vec = const((1, RW_WIDTH))
    return pl.pallas_call(
        functools.partial(_rwkv_prompt_kernel, tb=tb),
        grid=(t // tb,),
        in_specs=[pl.BlockSpec((tb, RW_SHIFT_WIDTH), lambda i: (i, 0)),
                  const((1, RW_SHIFT_WIDTH)), const((RW_NGROUP, n, n)), const((1, RW_SHIFT_WIDTH)),
                  vec, vec, const((2 * RW_LORA, 2 * RW_WIDTH)), vec, vec, vec, vec, vec,
                  const((RW_WIDTH, RW_WIDTH)), const((RW_CHUNK, RW_CHUNK)), const((RW_CHUNK, 128))],
        out_specs=[pl.BlockSpec((tb, RW_WIDTH), lambda i: (i, 0)), const((RW_NGROUP, n, n))],
        out_shape=[jax.ShapeDtypeStruct((t, RW_WIDTH), F32), jax.ShapeDtypeStruct((RW_NGROUP, n, n), F32)],
        scratch_shapes=[pltpu.VMEM((RW_NGROUP, n, n), F32), pltpu.VMEM((1, RW_SHIFT_WIDTH), F32)]
        + [pltpu.VMEM((tb, RW_WIDTH), F32)] * 7,
        compiler_params=_cparams(1),
        name="rwkv_prompt",
    )(proj, prev_row, s0_bd, rwp["mu"], rwp["w0"], rwp["a0"], rwp["w2a"], rwp["k_k"], rwp["k_a"], rwp["r_k"],
      rwp["ln_g"], rwp["ln_b"], gsum, lincl, ones)


def _state_to_bd(s):
    st = jnp.swapaxes(s, -1, -2).reshape(RW_NGROUP, RW_GROUP, RW_HEAD_DIM, RW_HEAD_DIM)
    eye = jnp.eye(RW_GROUP, dtype=s.dtype)
    bd = st[:, :, :, None, :] * eye[None, :, None, :, None]
    n = RW_GROUP * RW_HEAD_DIM
    return bd.reshape(RW_NGROUP, n, n)


def _bd_to_state(bd):
    n = RW_GROUP * RW_HEAD_DIM
    b5 = bd.reshape(RW_NGROUP, RW_GROUP, RW_HEAD_DIM, RW_GROUP, RW_HEAD_DIM)
    idx = jnp.arange(RW_GROUP)
    blocks = b5[:, idx, :, idx, :]
    blocks = jnp.swapaxes(blocks, 0, 1).reshape(RW_HEADS, RW_HEAD_DIM, RW_HEAD_DIM)
    return jnp.swapaxes(blocks, -1, -2)


def _rwkv_sample_prep_kernel(p_ref, prev_ref, mu_ref, w0_ref, a0_ref, w2a_ref, kk_ref, ka_ref, rk_ref, gsum_ref,
                             r_ref, w_ref, k_ref, v_ref, kko_ref, b_ref, bonus_ref):
    r, lw, k, v, kk, b, bonus = _rw_prep(p_ref[...], prev_ref[...], mu_ref[...], w0_ref[...], a0_ref[...],
                                         w2a_ref[...], kk_ref[...], ka_ref[...], rk_ref[...], gsum_ref[...])
    r_ref[...] = r
    w_ref[...] = jnp.exp(lw)
    k_ref[...] = k
    v_ref[...] = v
    kko_ref[...] = kk
    b_ref[...] = b
    bonus_ref[...] = bonus


def _rwkv_sample_state_kernel(s_ref, r_ref, w_ref, k_ref, kk_ref, b_ref, v_ref, bonus_ref, lng_ref, lnb_ref,
                              y_ref, so_ref):
    s = s_ref[...]
    sa = jnp.sum(s * kk_ref[...], axis=-1, keepdims=True)
    s = s * w_ref[...] - sa * b_ref[...] + v_ref[...] * k_ref[...]
    so_ref[...] = s
    y = jnp.sum(s * r_ref[...], axis=-1, keepdims=True)
    mean = jnp.mean(y, axis=-2, keepdims=True)
    d = y - mean
    var = jnp.mean(d * d, axis=-2, keepdims=True)
    y_ref[...] = d * lax.rsqrt(var + GN_EPS) * lng_ref[...] + lnb_ref[...] + bonus_ref[...]


def _rwkv_sample(proj, prev_rows, state, rwp):
    b = proj.shape[0]
    gsum, _, _ = _rw_consts()
    full = lambda shape: pl.BlockSpec(shape, lambda i: (0,) * len(shape))
    vec = full((1, RW_WIDTH))
    outs = pl.pallas_call(
        _rwkv_sample_prep_kernel,
        grid=(1,),
        in_specs=[pl.BlockSpec((b, RW_SHIFT_WIDTH), lambda i: (0, 0)),
                  full((b, RW_SHIFT_WIDTH)), full((1, RW_SHIFT_WIDTH)), vec, vec,
                  full((2 * RW_LORA, 2 * RW_WIDTH)), vec, vec, vec, full((RW_WIDTH, RW_WIDTH))],
        out_specs=[full((b, RW_WIDTH))] * 7,
        out_shape=[jax.ShapeDtypeStruct((b, RW_WIDTH), F32)] * 7,
        compiler_params=_cparams(1),
        name="rwkv_sample_prep",
    )(proj, prev_rows, rwp["mu"], rwp["w0"], rwp["a0"], rwp["w2a"], rwp["k_k"], rwp["k_a"], rwp["r_k"], gsum)
    r, w, k, v, kk, bb, bonus = outs
    as_row = lambda a: a.reshape(b, RW_HEADS, 1, RW_HEAD_DIM)
    as_col = lambda a: a.reshape(-1, RW_HEADS, RW_HEAD_DIM, 1)
    nb = SAMPLE_ROWS_PER_STEP
    row_spec = pl.BlockSpec((nb, RW_HEADS, 1, RW_HEAD_DIM), lambda i: (i, 0, 0, 0))
    col_spec = pl.BlockSpec((nb, RW_HEADS, RW_HEAD_DIM, 1), lambda i: (i, 0, 0, 0))
    par_spec = pl.BlockSpec((RW_HEADS, RW_HEAD_DIM, 1), lambda i: (0, 0, 0))
    s_spec = pl.BlockSpec((nb, RW_HEADS, RW_HEAD_DIM, RW_HEAD_DIM), lambda i: (i, 0, 0, 0))
    y, s_new = pl.pallas_call(
        _rwkv_sample_state_kernel,
        grid=(b // nb,),
        in_specs=[s_spec, row_spec, row_spec, row_spec, row_spec, row_spec, col_spec, col_spec, par_spec, par_spec],
        out_specs=[col_spec, s_spec],
        out_shape=[jax.ShapeDtypeStruct((b, RW_HEADS, RW_HEAD_DIM, 1), F32),
                   jax.ShapeDtypeStruct(state.shape, F32)],
        compiler_params=_cparams(1),
        name="rwkv_sample_state",
    )(state, as_row(r), as_row(w), as_row(k), as_row(kk), as_row(bb), as_col(v), as_col(bonus),
      rwp["ln_g"].reshape(RW_HEADS, RW_HEAD_DIM, 1), rwp["ln_b"].reshape(RW_HEADS, RW_HEAD_DIM, 1))
    return y.reshape(b, RW_WIDTH), s_new


def _mem_prompt_kernel(q_ref, mk_ref, mv_ref, o_ref):
    scale = MEM_HEAD_DIM ** -0.5
    for h in range(MEM_HEADS):
        sl = slice(h * MEM_HEAD_DIM, (h + 1) * MEM_HEAD_DIM)
        s = _dot_nt(q_ref[:, sl].astype(BF16), mk_ref[:, sl].astype(BF16)) * scale
        p = jnp.exp(s - jnp.max(s, axis=1, keepdims=True))
        l = jnp.sum(p, axis=1, keepdims=True)
        o_ref[:, sl] = _dot(p.astype(BF16), mv_ref[:, sl].astype(BF16)) / l


def _mem_prompt(proj, mk, mv, tm):
    t = proj.shape[0]
    m = mk.shape[0]
    return pl.pallas_call(
        _mem_prompt_kernel,
        grid=(t // tm,),
        in_specs=[pl.BlockSpec((tm, MEM_WIDTH), lambda i: (i, COL_MQ // MEM_WIDTH)),
                  pl.BlockSpec((m, MEM_WIDTH), lambda i: (0, 0)),
                  pl.BlockSpec((m, MEM_WIDTH), lambda i: (0, 0))],
        out_specs=pl.BlockSpec((tm, MEM_WIDTH), lambda i: (i, 0)),
        out_shape=jax.ShapeDtypeStruct((t, MEM_WIDTH), F32),
        compiler_params=_cparams(1),
        name="mem_prompt",
    )(proj, mk, mv)


def _mem_sample_kernel(q_ref, mk_ref, mv_ref, o_ref):
    scale = MEM_HEAD_DIM ** -0.5
    rowh = lax.broadcasted_iota(jnp.int32, (8, MEM_WIDTH), 0)
    laneh = lax.broadcasted_iota(jnp.int32, (8, MEM_WIDTH), 1) // MEM_HEAD_DIM
    own = rowh == laneh
    for j in range(q_ref.shape[0]):
        q = q_ref[j]
        qblk = jnp.where(own, q, 0.0).astype(BF16)
        s = _dot_nt(qblk, mk_ref[j].astype(BF16)) * scale
        p = jnp.exp(s - jnp.max(s, axis=1, keepdims=True))
        l = jnp.sum(p, axis=1, keepdims=True)
        o = _dot(p.astype(BF16), mv_ref[j].astype(BF16)) / l
        o_ref[j] = jnp.sum(jnp.where(own, o, 0.0), axis=0, keepdims=True)


def _mem_sample(q, mk, mv):
    b, m, _ = mk.shape
    nb = SAMPLE_ROWS_PER_STEP
    q_spec = pl.BlockSpec((nb, 1, MEM_WIDTH), lambda i: (i, 0, 0))
    kv_spec = pl.BlockSpec((nb, m, MEM_WIDTH), lambda i: (i, 0, 0))
    out = pl.pallas_call(
        _mem_sample_kernel,
        grid=(b // nb,),
        in_specs=[q_spec, kv_spec, kv_spec],
        out_specs=q_spec,
        out_shape=jax.ShapeDtypeStruct((b, 1, MEM_WIDTH), F32),
        compiler_params=_cparams(1),
        name="mem_sample",
    )(q.reshape(b, 1, MEM_WIDTH), mk, mv)
    return out.reshape(b, MEM_WIDTH)


def _merge_kernel(x_ref, gl_ref, osb_ref, sz_ref, orw_ref, rz_ref, omem_ref, mz_ref,
                  wsb_ref, wrw_ref, wmem_ref, wo_ref, fg_ref, o_ref, *, final):
    a_sb = (osb_ref[...] * _silu(sz_ref[...])).astype(BF16)
    a_rw = (orw_ref[...] * _silu(rz_ref[...])).astype(BF16)
    a_mem = (omem_ref[...] * _silu(mz_ref[...])).astype(BF16)
    merged = (_sigmoid(gl_ref[:, 0:D_MODEL]) * _dot(a_sb, wsb_ref[...])
              + _sigmoid(gl_ref[:, D_MODEL:2 * D_MODEL]) * _dot(a_rw, wrw_ref[...])
              + _sigmoid(gl_ref[:, 2 * D_MODEL:3 * D_MODEL]) * _dot(a_mem, wmem_ref[...]))
    y = x_ref[...] + _dot(merged.astype(BF16), wo_ref[...])
    if final:
        ms = jnp.mean(y * y, axis=-1, keepdims=True)
        y = y * lax.rsqrt(ms + RMS_EPS) * fg_ref[...]
    o_ref[...] = y


def _merge(x, proj, o_sb, o_rw, o_mem, w_sb, w_rw, w_mem, w_o, final_g, final, tm, name):
    t = x.shape[0]
    col = lambda c: pl.BlockSpec((tm, SB_WIDTH), lambda i: (i, c // SB_WIDTH))
    act = pl.BlockSpec((tm, SB_WIDTH), lambda i: (i, 0))
    wspec = pl.BlockSpec((SB_WIDTH, D_MODEL), lambda i: (0, 0))
    return pl.pallas_call(
        functools.partial(_merge_kernel, final=final),
        grid=(t // tm,),
        in_specs=[pl.BlockSpec((tm, D_MODEL), lambda i: (i, 0)),
                  pl.BlockSpec((tm, 3 * D_MODEL), lambda i: (i, 0)),
                  act, col(COL_SZ), act, col(COL_RZ), act, col(COL_MZ),
                  wspec, wspec, wspec, pl.BlockSpec((D_MODEL, D_MODEL), lambda i: (0, 0)),
                  pl.BlockSpec((1, D_MODEL), lambda i: (0, 0))],
        out_specs=pl.BlockSpec((tm, D_MODEL), lambda i: (i, 0)),
        out_shape=jax.ShapeDtypeStruct((t, D_MODEL), F32),
        compiler_params=_cparams(1),
        name=name,
    )(x, proj, o_sb, proj, o_rw, proj, o_mem, proj, w_sb, w_rw, w_mem, w_o, final_g.reshape(1, D_MODEL))


def _split_w_in(w_in):
    sq, sk, sv, sz, rp, rz, mq, mz, gl = jnp.split(
        w_in, np.cumsum([SB_WIDTH] * 4 + [RW_SHIFT_WIDTH, RW_WIDTH, MEM_WIDTH, MEM_WIDTH]).tolist(), axis=1)
    w_act = jnp.concatenate([gl, sq, sz, rz, mq, mz], axis=1).astype(BF16)
    return w_act, rp.astype(BF16), sk.T.astype(BF16), sv.T.astype(BF16)


def _pages_from_t(xt, page):
    t = xt.shape[1]
    x = xt.reshape(SB_HEADS, SB_HEAD_DIM, t // page, page)
    return jnp.transpose(x, (2, 3, 0, 1))[None]


def kernel(x_prompt, x_sample, cache_sb_k, cache_sb_v, cache_mem_k, cache_mem_v, state_wkv, state_shift, page_table, mem_prompt, norm_g, w_in, sb_bias, shift_mu, rw_w0, rw_w2, rw_a0, rw_a2, rw_k_k, rw_k_a, rw_r_k, rw_ln_g, rw_ln_b, mem_norm_g, w_mem_kv, w_bo_sb, w_bo_rw, w_bo_mem, w_o, final_norm_g):
    depth = w_in.shape[0]
    bp, tp, _ = x_prompt.shape
    bs, ts, _ = x_sample.shape
    assert bp == 1 and ts == 1
    page = cache_sb_k.shape[2]
    n_mem = mem_prompt.shape[1]
    sb_scale = SB_HEAD_DIM ** -0.5

    hp = x_prompt.reshape(tp, D_MODEL)
    hs = x_sample.reshape(bs, D_MODEL)
    mem = mem_prompt.reshape(n_mem, D_MODEL)
    k_pool_t = jnp.transpose(cache_sb_k, (0, 1, 3, 4, 2))
    v_pool_t = jnp.transpose(cache_sb_v, (0, 1, 3, 4, 2))
    outs = {k: [] for k in ("sbk_p", "sbv_p", "mk_p", "mv_p", "wkv_p", "shift_p", "sbk_s", "sbv_s", "wkv_s", "shift_s")}
    for l in range(depth):
        last = l == depth - 1
        lp = dict(shift_mu=shift_mu[l], rw_w0=rw_w0[l], rw_w2=rw_w2[l], rw_a0=rw_a0[l], rw_a2=rw_a2[l],
                  rw_k_k=rw_k_k[l], rw_k_a=rw_k_a[l], rw_r_k=rw_r_k[l], rw_ln_g=rw_ln_g[l], rw_ln_b=rw_ln_b[l])
        rwp = _rw_params(lp)
        w_act, w_rp, w_kt, w_vt = _split_w_in(w_in[l])
        w_sb, w_rw, w_mem, w_out = (w.astype(BF16) for w in (w_bo_sb[l], w_bo_rw[l], w_bo_mem[l], w_o[l]))

        act, rp, kt, vt = _in_proj(hp, norm_g[l], w_act, w_rp, w_kt, w_vt, 512, "proj_prompt")
        kv = _norm_proj(mem, mem_norm_g[l], w_mem_kv[l].astype(BF16), n_mem, MEM_WIDTH, "mem_kv")
        mk, mv = kv[:, :MEM_WIDTH], kv[:, MEM_WIDTH:]
        o_sb = _sb_prompt(act, kt.reshape(SB_HEADS, SB_HEAD_DIM, tp), vt.reshape(SB_HEADS, SB_HEAD_DIM, tp),
                          sb_bias[l], 512, 1024)
        s0 = _state_to_bd(jnp.zeros((RW_HEADS, RW_HEAD_DIM, RW_HEAD_DIM), F32))
        o_rw, s_bd = _rwkv_prompt(rp, jnp.zeros((1, RW_SHIFT_WIDTH), F32), s0, rwp, 512)
        o_mem = _mem_prompt(act, mk, mv, 512)
        hp = _merge(hp, act, o_sb, o_rw, o_mem, w_sb, w_rw, w_mem, w_out, final_norm_g, last, 256, "merge_prompt")
        outs["sbk_p"].append(_pages_from_t(kt, page))
        outs["sbv_p"].append(_pages_from_t(vt, page))
        outs["mk_p"].append(mk.reshape(bp, n_mem, MEM_HEADS, MEM_HEAD_DIM))
        outs["mv_p"].append(mv.reshape(bp, n_mem, MEM_HEADS, MEM_HEAD_DIM))
        outs["wkv_p"].append(_bd_to_state(s_bd).reshape(bp, RW_HEADS, RW_HEAD_DIM, RW_HEAD_DIM))
        outs["shift_p"].append(rp[tp - 1:tp])

        act_s, rp_s, kt_s, vt_s = _in_proj(hs, norm_g[l], w_act, w_rp, w_kt, w_vt, bs, "proj_sample")
        o_sb_s = _sb_sample(act_s[:, COL_SQ:COL_SQ + SB_WIDTH] * sb_scale, sb_bias[l], k_pool_t, v_pool_t, l,
                            page_table, 32)
        o_rw_s, s_new = _rwkv_sample(rp_s, state_shift[l], state_wkv[l], rwp)
        o_mem_s = _mem_sample(act_s[:, COL_MQ:COL_MQ + MEM_WIDTH],
                              cache_mem_k[l].reshape(bs, n_mem, MEM_WIDTH),
                              cache_mem_v[l].reshape(bs, n_mem, MEM_WIDTH))
        hs = _merge(hs, act_s, o_sb_s, o_rw_s, o_mem_s, w_sb, w_rw, w_mem, w_out, final_norm_g, last, bs,
                    "merge_sample")
        outs["sbk_s"].append(kt_s.T.reshape(bs, ts, SB_HEADS, SB_HEAD_DIM))
        outs["sbv_s"].append(vt_s.T.reshape(bs, ts, SB_HEADS, SB_HEAD_DIM))
        outs["wkv_s"].append(s_new)
        outs["shift_s"].append(rp_s)

    st = {k: jnp.stack(v) for k, v in outs.items()}
    return (hp.reshape(bp, tp, D_MODEL), hs.reshape(bs, ts, D_MODEL),
            st["sbk_p"], st["sbv_p"], st["mk_p"], st["mv_p"], st["wkv_p"], st["shift_p"],
            st["sbk_s"], st["sbv_s"], st["wkv_s"], st["shift_s"])
```

```python
import functools

import numpy as np
import jax
import jax.numpy as jnp
from jax import lax
from jax.experimental import pallas as pl
from jax.experimental.pallas import tpu as pltpu

F32 = jnp.float32
BF16 = jnp.bfloat16

D_MODEL = 1024
SB_HEADS = 8
SB_HEAD_DIM = 64
SB_WIDTH = SB_HEADS * SB_HEAD_DIM
RW_HEADS = 8
RW_HEAD_DIM = 64
RW_WIDTH = RW_HEADS * RW_HEAD_DIM
RW_LORA = 64
RW_SHIFT_WIDTH = 3 * RW_WIDTH + 2 * RW_LORA
MEM_HEADS = 4
MEM_HEAD_DIM = 128
MEM_WIDTH = MEM_HEADS * MEM_HEAD_DIM
GN_EPS = 64e-5
RMS_EPS = 1e-6
C_IN = 4 * SB_WIDTH + RW_SHIFT_WIDTH + RW_WIDTH + 2 * MEM_WIDTH + 3 * D_MODEL

COL_GATE = 0
COL_SQ = 3 * D_MODEL
COL_SZ = COL_SQ + SB_WIDTH
COL_RZ = COL_SZ + SB_WIDTH
COL_MQ = COL_RZ + RW_WIDTH
COL_MZ = COL_MQ + MEM_WIDTH
ACT_WIDTH = COL_MZ + MEM_WIDTH
ACT_TILES = 2
assert ACT_WIDTH + RW_SHIFT_WIDTH + 2 * SB_WIDTH == C_IN and ACT_WIDTH % (ACT_TILES * 128) == 0
LOG2E = 1.4426950408889634
SB_KPAD = 128

VMEM_LIMIT_BYTES = 56 * 1024 * 1024
MXU_TILE = 256

RW_CHUNK = 64
RW_GROUP = MXU_TILE // RW_CHUNK
RW_NGROUP = RW_HEADS // RW_GROUP
RW_LOCKSTEP = 4
SB_SUB = MXU_TILE
SAMPLE_ROWS_PER_STEP = 8


def _cparams(n_grid):
    return pltpu.CompilerParams(dimension_semantics=("arbitrary",) * n_grid,
                                vmem_limit_bytes=VMEM_LIMIT_BYTES)


def _sigmoid(x):
    return 1.0 / (1.0 + jnp.exp(-x))


def _softplus(x):
    return jnp.maximum(x, 0.0) + jnp.log(1.0 + jnp.exp(-jnp.abs(x)))


def _silu(x):
    return x * _sigmoid(x)


def _split_bf16(x):
    hi = x.astype(BF16)
    lo = (x - hi.astype(F32)).astype(BF16)
    return hi, lo


def _dot(a, b):
    return jnp.dot(a, b, preferred_element_type=F32)


def _dot_nt(a, b):
    return lax.dot_general(a, b, (((1,), (1,)), ((), ())), preferred_element_type=F32)


def _dot_tn(a, b):
    return lax.dot_general(a, b, (((0,), (0,)), ((), ())), preferred_element_type=F32)


def _dot_hilo(x, w01):
    hi, lo = _split_bf16(x)
    return _dot(hi, w01) + _dot(lo, w01)


def _norm_proj_kernel(x_ref, g_ref, w_ref, o_ref, h_ref):
    @pl.when(pl.program_id(1) == 0)
    def _():
        x = x_ref[...]
        ms = jnp.mean(x * x, axis=-1, keepdims=True)
        h_ref[...] = (x * lax.rsqrt(ms + RMS_EPS) * g_ref[...]).astype(BF16)

    o_ref[...] = _dot(h_ref[...], w_ref[...])


def _norm_proj(x, g, w, tm, tn, name):
    t, d = x.shape
    n = w.shape[1]
    return pl.pallas_call(
        _norm_proj_kernel,
        grid=(t // tm, n // tn),
        in_specs=[pl.BlockSpec((tm, d), lambda i, j: (i, 0)),
                  pl.BlockSpec((1, d), lambda i, j: (0, 0)),
                  pl.BlockSpec((d, tn), lambda i, j: (0, j))],
        out_specs=pl.BlockSpec((tm, tn), lambda i, j: (i, j)),
        out_shape=jax.ShapeDtypeStruct((t, n), F32),
        scratch_shapes=[pltpu.VMEM((tm, d), BF16)],
        compiler_params=_cparams(2),
        name=name,
    )(x, g.reshape(1, d), w)


def _in_proj_kernel(x_ref, g_ref, *refs):
    wa_refs = refs[:ACT_TILES]
    wrp_ref, wkt_ref, wvt_ref, act_ref, rp_ref, kt_ref, vt_ref, h_ref = refs[ACT_TILES:]
    j = pl.program_id(1)

    @pl.when(j == 0)
    def _():
        x = x_ref[...]
        ms = jnp.mean(x * x, axis=-1, keepdims=True)
        h_ref[...] = (x * lax.rsqrt(ms + RMS_EPS) * g_ref[...]).astype(BF16)

    for tile, wa_ref in enumerate(wa_refs):
        @pl.when(j == tile)
        def _():
            act_ref[...] = _dot(h_ref[...], wa_ref[...])

    @pl.when(j == ACT_TILES)
    def _():
        rp_ref[...] = _dot(h_ref[...], wrp_ref[...])

    @pl.when(j == ACT_TILES + 1)
    def _():
        h = h_ref[...]
        kt_ref[...] = _dot_nt(wkt_ref[...], h)
        vt_ref[...] = _dot_nt(wvt_ref[...], h)


def _in_proj(x, g, w_act, w_rp, w_kt, w_vt, tm, name):
    t, d = x.shape
    tn = ACT_WIDTH // ACT_TILES
    last_act = ACT_TILES - 1
    const = lambda shape: pl.BlockSpec(shape, lambda i, j: (0, 0), pipeline_mode=pl.Buffered(1))
    return pl.pallas_call(
        _in_proj_kernel,
        grid=(t // tm, ACT_TILES + 2),
        in_specs=[pl.BlockSpec((tm, d), lambda i, j: (i, 0)), const((1, d))]
        + [pl.BlockSpec((d, tn), lambda i, j, tile=tile: (0, tile), pipeline_mode=pl.Buffered(1))
           for tile in range(ACT_TILES)]
        + [const((d, RW_SHIFT_WIDTH)), const((SB_WIDTH, d)), const((SB_WIDTH, d))],
        out_specs=[pl.BlockSpec((tm, tn), lambda i, j: (i, jnp.minimum(j, last_act))),
                   pl.BlockSpec((tm, RW_SHIFT_WIDTH), lambda i, j: (i, 0)),
                   pl.BlockSpec((SB_WIDTH, tm), lambda i, j: (0, i)),
                   pl.BlockSpec((SB_WIDTH, tm), lambda i, j: (0, i))],
        out_shape=[jax.ShapeDtypeStruct((t, ACT_WIDTH), F32), jax.ShapeDtypeStruct((t, RW_SHIFT_WIDTH), F32),
                   jax.ShapeDtypeStruct((SB_WIDTH, t), F32), jax.ShapeDtypeStruct((SB_WIDTH, t), F32)],
        scratch_shapes=[pltpu.VMEM((tm, d), BF16)],
        compiler_params=_cparams(2),
        name=name,
    )(x, g.reshape(1, d), *([w_act] * ACT_TILES), w_rp, w_kt, w_vt)


def _sb_prompt_kernel(qi_ref, kj_ref, bias_ref, q_ref, kt_ref, vt_ref, tri_ref, o_ref, q_scr, acc_ref, carry_ref,
                      z_scr, *, bq, bk):
    s = pl.program_id(0)
    qi = qi_ref[s]
    kj = kj_ref[s]
    diag = kj == (qi * bq) // bk
    pad = SB_KPAD - SB_HEAD_DIM

    @pl.when(diag)
    def _():
        acc_ref[...] = jnp.zeros_like(acc_ref)
        carry_ref[...] = jnp.zeros_like(carry_ref)
        lane = lax.broadcasted_iota(jnp.int32, (bq, pad), 1)
        for h in range(SB_HEADS):
            qh = q_ref[:, h * SB_HEAD_DIM:(h + 1) * SB_HEAD_DIM] * (SB_HEAD_DIM ** -0.5 * LOG2E)
            ext = jnp.where(lane == 0, bias_ref[0, h],
                            jnp.where(lane == 1, bias_ref[1, h], jnp.where(lane == 2, bias_ref[2, h], 0.0)))
            q_scr[h] = jnp.concatenate([qh, ext], axis=1).astype(BF16)

    ones_rows = (lax.broadcasted_iota(jnp.int32, (pad, SB_SUB), 0) < 3).astype(BF16)

    def logits(h, kb):
        ks = slice(kb * SB_SUB, (kb + 1) * SB_SUB)
        return _dot(q_scr[h], jnp.concatenate([kt_ref[h, :, ks].astype(BF16), ones_rows], axis=0))

    def run(row_off):
        kbs = []
        for kb in reversed(range(bk // SB_SUB)):
            if row_off is None or (kb + 1) * SB_SUB <= row_off:
                kbs.append((kb, False))
            elif kb * SB_SUB < row_off + bq:
                kbs.append((kb, True))
        n_sub = len(kbs)
        assert n_sub % 2 == 0
        z_scr[0] = logits(0, kbs[0][0])

        def head(h, _):
            tri = tri_ref[...]
            c = carry_ref[h]
            acc = acc_ref[h]
            for i, (kb, masked) in enumerate(kbs):
                ks = slice(kb * SB_SUB, (kb + 1) * SB_SUB)
                z = z_scr[i % 2]
                z_scr[(i + 1) % 2] = (logits(h, kbs[i + 1][0]) if i + 1 < n_sub
                                      else logits((h + 1) % SB_HEADS, kbs[0][0]))
                neg_abs = pltpu.bitcast(pltpu.bitcast(z, jnp.uint32) | jnp.uint32(0x80000000), F32)
                sp = jnp.maximum(z, 0.0) + jnp.log(1.0 + jnp.exp2(neg_abs)) * LOG2E
                if masked:
                    row = lax.broadcasted_iota(jnp.int32, (bq, SB_SUB), 0) + row_off
                    col = lax.broadcasted_iota(jnp.int32, (bq, SB_SUB), 1) + kb * SB_SUB
                    vis = col < row
                    sp = jnp.where(vis, sp, 0.0)
                within = _dot(sp.astype(BF16), tri)
                w = jnp.exp2(z - sp - within - c)
                if masked:
                    w = jnp.where(vis, w, 0.0)
                acc = acc + _dot_nt(w.astype(BF16), vt_ref[h, :, ks].astype(BF16))
                c = c + (within[:, :1] + sp[:, :1])
            carry_ref[h] = c
            acc_ref[h] = acc
            return 0

        lax.fori_loop(0, SB_HEADS, head, 0)

    for row_off in range(0, bk, bq):
        @pl.when(jnp.logical_and(diag, (qi * bq) % bk == row_off))
        def _():
            run(row_off)

    @pl.when(jnp.logical_not(diag))
    def _():
        run(None)

    @pl.when(kj == 0)
    def _():
        for h in range(SB_HEADS):
            o_ref[:, h * SB_HEAD_DIM:(h + 1) * SB_HEAD_DIM] = acc_ref[h]


def _sb_prompt(act, kt, vt, bias, bq, bk):
    t = act.shape[0]
    h, d = SB_HEADS, SB_HEAD_DIM
    nq = t // bq
    assert bk % bq == 0 and t % bk == 0
    first = [(i * bq) // bk for i in range(nq)]
    qi = np.concatenate([np.full(first[i] + 1, i) for i in range(nq)]).astype(np.int32)
    kj = np.concatenate([np.arange(first[i], -1, -1) for i in range(nq)]).astype(np.int32)
    tri = (np.arange(SB_SUB)[:, None] > np.arange(SB_SUB)[None, :]).astype(np.float32)
    b2 = bias.astype(F32) * LOG2E
    b_hi = b2.astype(BF16).astype(F32)
    b_mid = (b2 - b_hi).astype(BF16).astype(F32)
    b_lo = (b2 - b_hi - b_mid).astype(BF16).astype(F32)
    grid_spec = pltpu.PrefetchScalarGridSpec(
        num_scalar_prefetch=2,
        grid=(len(qi),),
        in_specs=[pl.BlockSpec(memory_space=pltpu.SMEM),
                  pl.BlockSpec((bq, SB_WIDTH), lambda s, qi, kj: (qi[s], COL_SQ // SB_WIDTH)),
                  pl.BlockSpec((h, d, bk), lambda s, qi, kj: (0, 0, kj[s])),
                  pl.BlockSpec((h, d, bk), lambda s, qi, kj: (0, 0, kj[s])),
                  pl.BlockSpec((SB_SUB, SB_SUB), lambda s, qi, kj: (0, 0))],
        out_specs=pl.BlockSpec((bq, SB_WIDTH), lambda s, qi, kj: (qi[s], 0)),
        scratch_shapes=[pltpu.VMEM((h, bq, SB_KPAD), BF16), pltpu.VMEM((h, bq, d), F32),
                        pltpu.VMEM((h, bq, 1), F32), pltpu.VMEM((2, bq, SB_SUB), F32)],
    )
    return pl.pallas_call(
        functools.partial(_sb_prompt_kernel, bq=bq, bk=bk),
        grid_spec=grid_spec,
        out_shape=jax.ShapeDtypeStruct((t, SB_WIDTH), F32),
        compiler_params=_cparams(1),
        name="sb_prompt",
    )(jnp.asarray(qi), jnp.asarray(kj), jnp.stack([b_hi, b_mid, b_lo]), act, kt, vt, jnp.asarray(tri, BF16))


def _sb_sample_kernel(pt_ref, q_ref, bias_ref, *refs, pp, n_steps):
    k_refs = refs[:pp]
    v_refs = refs[pp:2 * pp]
    tri_ref, o_ref, acc_ref, carry_ref, z_scr, w_scr = refs[2 * pp:]
    g = pl.program_id(1)

    @pl.when(g == 0)
    def _():
        acc_ref[...] = jnp.zeros_like(acc_ref)
        carry_ref[...] = jnp.zeros_like(carry_ref)

    q = q_ref[0]
    for i in range(pp):
        for h in range(SB_HEADS):
            z_scr[i * SB_HEADS + h:i * SB_HEADS + h + 1, :] = jnp.sum(k_refs[i][0, 0, h] * q[h], axis=0, keepdims=True)
    z = z_scr[...] + jnp.concatenate([bias_ref[...]] * pp, axis=0)
    sp = _softplus(z)
    within = _dot_hilo(sp, tri_ref[...])
    tot = jnp.sum(sp, axis=1, keepdims=True)
    c = carry_ref[...]
    cs = [None] * pp
    for i in reversed(range(pp)):
        cs[i] = c
        c = c + tot[i * SB_HEADS:(i + 1) * SB_HEADS]
    carry_ref[...] = c
    w_scr[...] = jnp.exp(z - sp - within - jnp.concatenate(cs, axis=0))
    for h in range(SB_HEADS):
        a = acc_ref[h]
        for i in range(pp):
            a = a + v_refs[i][0, 0, h] * w_scr[i * SB_HEADS + h:i * SB_HEADS + h + 1, :]
        acc_ref[h] = a

    @pl.when(g == n_steps - 1)
    def _():
        o_ref[0] = jnp.sum(acc_ref[...], axis=2, keepdims=True)


def _sb_sample(q, bias, k_pool, v_pool, layer, page_table, pp):
    b = q.shape[0]
    n_pages = page_table.shape[1]
    page = k_pool.shape[-1]
    n_steps = n_pages // pp
    tri = (np.arange(page)[:, None] > np.arange(page)[None, :]).astype(np.float32)

    def page_spec(i):
        return pl.BlockSpec((1, 1, SB_HEADS, SB_HEAD_DIM, page),
                            lambda bi, g, pt: (layer, pt[bi, (n_steps - 1 - g) * pp + i], 0, 0, 0))

    q_spec = pl.BlockSpec((1, SB_HEADS, SB_HEAD_DIM, 1), lambda bi, g, pt: (bi, 0, 0, 0))
    grid_spec = pltpu.PrefetchScalarGridSpec(
        num_scalar_prefetch=1,
        grid=(b, n_steps),
        in_specs=[q_spec, pl.BlockSpec((SB_HEADS, 1), lambda bi, g, pt: (0, 0))]
        + [page_spec(i) for i in range(pp)] + [page_spec(i) for i in range(pp)]
        + [pl.BlockSpec((page, page), lambda bi, g, pt: (0, 0))],
        out_specs=q_spec,
        scratch_shapes=[pltpu.VMEM((SB_HEADS, SB_HEAD_DIM, page), F32), pltpu.VMEM((SB_HEADS, 1), F32),
                        pltpu.VMEM((SB_HEADS * pp, page), F32), pltpu.VMEM((SB_HEADS * pp, page), F32)],
    )
    out = pl.pallas_call(
        functools.partial(_sb_sample_kernel, pp=pp, n_steps=n_steps),
        grid_spec=grid_spec,
        out_shape=jax.ShapeDtypeStruct((b, SB_HEADS, SB_HEAD_DIM, 1), F32),
        compiler_params=_cparams(2),
        name="sb_sample",
    )(page_table, q.reshape(b, SB_HEADS, SB_HEAD_DIM, 1), bias.reshape(SB_HEADS, 1),
      *([k_pool] * pp), *([v_pool] * pp), jnp.asarray(tri, BF16))
    return out.reshape(b, SB_WIDTH)


def _rw_prep(p, prev, mu, w0, a0, w2a, k_k, k_a, r_k, gsum):
    u = p + mu * (prev - p)
    r = u[:, 0:RW_WIDTH]
    k = u[:, RW_WIDTH:2 * RW_WIDTH]
    v = u[:, 2 * RW_WIDTH:3 * RW_WIDTH]
    x2 = u[:, 3 * RW_WIDTH:]
    lane = lax.broadcasted_iota(jnp.int32, x2.shape, 1)
    x2 = jnp.where(lane < RW_LORA, jnp.tanh(x2), x2)
    d = _dot(x2.astype(BF16), w2a)
    w_log = -_softplus(-(w0 + d[:, :RW_WIDTH])) - 0.5
    lw = -jnp.exp(w_log)
    a = _sigmoid(a0 + d[:, RW_WIDTH:])
    kk = k * k_k
    kk = kk / jnp.maximum(jnp.sqrt(_dot_hilo(kk * kk, gsum)), 1e-12)
    k = k * (1.0 + (a - 1.0) * k_a)
    bonus = _dot_hilo(r * k * r_k, gsum) * v
    return r, lw, k, v, kk, kk * a, bonus


def _neumann_inverse(a_list, eye, bdiag):
    x = [eye - a for a in a_list]
    p = [_dot(a.astype(BF16), bdiag(a)) for a in a_list]
    order = 2
    while 2 * order < RW_CHUNK:
        pb = [bdiag(pi) for pi in p]
        x = [xi + _dot(xi.astype(BF16), pbi) for xi, pbi in zip(x, pb)]
        p = [_dot(pi.astype(BF16), pbi) for pi, pbi in zip(p, pb)]
        order *= 2
    return [xi + _dot(xi.astype(BF16), bdiag(pi)) for xi, pi in zip(x, p)]


def _rwkv_prompt_kernel(p_ref, prev_ref, s0_ref, mu_ref, w0_ref, a0_ref, w2a_ref, kk_ref, ka_ref, rk_ref,
                        lng_ref, lnb_ref, gsum_ref, lincl_ref, ones_ref,
                        y_ref, sout_ref,
                        s_scr, last_scr, r_scr, lw_scr, k_scr, v_scr, kk_scr, b_scr, y_scr, *, tb):
    i = pl.program_id(0)

    @pl.when(i == 0)
    def _():
        s_scr[...] = s0_ref[...]
        last_scr[...] = prev_ref[...]

    p = p_ref[...]
    row0 = lax.broadcasted_iota(jnp.int32, p.shape, 0) == 0
    prev = jnp.where(row0, last_scr[...], pltpu.roll(p, 1, axis=0))
    last_scr[...] = p[tb - 1:tb, :]
    gsum = gsum_ref[...]
    r, lw, k, v, kk, b, bonus = _rw_prep(p, prev, mu_ref[...], w0_ref[...], a0_ref[...], w2a_ref[...],
                                         kk_ref[...], ka_ref[...], rk_ref[...], gsum)
    r_scr[...] = r
    lw_scr[...] = lw
    k_scr[...] = k
    v_scr[...] = v
    kk_scr[...] = kk
    b_scr[...] = b
    y_ref[...] = bonus

    n = RW_GROUP * RW_CHUNK
    row = lax.broadcasted_iota(jnp.int32, (n, n), 0)
    col = lax.broadcasted_iota(jnp.int32, (n, n), 1)
    same_head = (row // RW_CHUNK) == (col // RW_HEAD_DIM)
    trow = lax.broadcasted_iota(jnp.int32, (RW_CHUNK, n), 0)
    tcol = lax.broadcasted_iota(jnp.int32, (RW_CHUNK, n), 1) % RW_CHUNK
    strict = tcol < trow
    incl = tcol <= trow
    eye = (tcol == trow).astype(F32)

    def bdiag(x):
        return jnp.where(same_head, jnp.concatenate([x] * RW_GROUP, axis=0), 0.0).astype(BF16)

    groups = range(RW_NGROUP)
    sls = [slice(g * n, (g + 1) * n) for g in groups]

    def decayed(ci):
        rows = pl.ds(pl.multiple_of(ci * RW_CHUNK, RW_CHUNK), RW_CHUNK)
        lwc = lw_scr[rows, :]
        lhi, llo = _split_bf16(lwc)
        lincl = lincl_ref[...]
        cum = _dot(lincl, lhi) + _dot(lincl, llo)
        cum_end = cum[RW_CHUNK - 1:RW_CHUNK, :]
        ones = ones_ref[...]
        g_end_col = jnp.exp(_dot_tn(lhi, ones) + _dot_tn(llo, ones))
        g_inv = jnp.exp(-cum)
        g_out = jnp.exp(cum_end - cum)
        rc = r_scr[rows, :] * jnp.exp(cum)
        kkc = kk_scr[rows, :] * jnp.exp(cum - lwc)
        bc = b_scr[rows, :]
        kc = k_scr[rows, :]
        return dict(rows=rows, g_end_col=g_end_col, rc=rc, kkc=kkc, vc=v_scr[rows, :],
                    bt=bc * g_inv, kt=kc * g_inv, bh=bc * g_out, kh=kc * g_out)

    def chunks(ci, _):
        cs = [decayed(ci * RW_LOCKSTEP + j) for j in range(RW_LOCKSTEP)]
        probs = [(c, sl) for c in cs for sl in sls]
        kr = [jnp.concatenate([c["kkc"][:, sl], c["rc"][:, sl]], axis=0).astype(BF16) for c, sl in probs]
        m_b = [_dot_nt(kr[i], bdiag(c["bt"][:, sl])) for i, (c, sl) in enumerate(probs)]
        m_k = [_dot_nt(kr[i], bdiag(c["kt"][:, sl])) for i, (c, sl) in enumerate(probs)]
        t_inv = _neumann_inverse([jnp.where(strict, m[:RW_CHUNK], 0.0) for m in m_b], eye, bdiag)
        ap_k = [jnp.concatenate([jnp.where(strict, m[:RW_CHUNK], 0.0), jnp.where(incl, m[RW_CHUNK:], 0.0)],
                                axis=0).astype(BF16) for m in m_k]
        from_v = [_dot(ap_k[i], bdiag(c["vc"][:, sl])) for i, (c, sl) in enumerate(probs)]
        p_b = [jnp.where(incl, m[RW_CHUNK:], 0.0).astype(BF16) for m in m_b]
        for j, c in enumerate(cs):
            ids = [j * RW_NGROUP + g for g in groups]
            s_old = [s_scr[g] for g in groups]
            from_s = [_dot(kr[i], s_old[g].astype(BF16)) for g, i in zip(groups, ids)]
            u = [-_dot(t_inv[i].astype(BF16), bdiag(from_s[g][:RW_CHUNK] + from_v[i][:RW_CHUNK]))
                 for g, i in zip(groups, ids)]
            for g, i in zip(groups, ids):
                c_y = from_s[g][RW_CHUNK:] + _dot(p_b[i], bdiag(u[g])) + from_v[i][RW_CHUNK:]
                y_scr[c["rows"], sls[g]] = c_y
            upd = [_dot_tn(jnp.concatenate([c["bh"][:, sls[g]], c["kh"][:, sls[g]]], axis=0).astype(BF16),
                           jnp.concatenate([u[g], c["vc"][:, sls[g]]], axis=0).astype(BF16)) for g in groups]
            for g in groups:
                g_end = jnp.concatenate([c["g_end_col"][sls[g], :]] * (n // 128), axis=1)
                s_scr[g] = g_end * s_old[g] + jnp.where(same_head, upd[g], 0.0)
        return 0

    lax.fori_loop(0, tb // (RW_CHUNK * RW_LOCKSTEP), chunks, 0)

    y = y_scr[...]
    inv_n = 1.0 / RW_HEAD_DIM
    mean = _dot_hilo(y, gsum) * inv_n
    d = y - mean
    var = _dot_hilo(d * d, gsum) * inv_n
    y_ref[...] = d * lax.rsqrt(var + GN_EPS) * lng_ref[...] + lnb_ref[...] + y_ref[...]

    @pl.when(i == pl.num_programs(0) - 1)
    def _():
        sout_ref[...] = s_scr[...]


def _rw_consts():
    hd = np.arange(RW_WIDTH) // RW_HEAD_DIM
    gsum = (hd[:, None] == hd[None, :]).astype(np.float32)
    lincl = (np.arange(RW_CHUNK)[:, None] >= np.arange(RW_CHUNK)[None, :]).astype(np.float32)
    ones = np.ones((RW_CHUNK, 128), np.float32)
    return jnp.asarray(gsum, BF16), jnp.asarray(lincl, BF16), jnp.asarray(ones, BF16)


def _rw_params(lp):
    row = lambda a: a.reshape(1, -1)
    z = jnp.zeros((RW_LORA, RW_WIDTH), F32)
    w2a = jnp.concatenate([jnp.concatenate([lp["rw_w2"], z], axis=1),
                           jnp.concatenate([z, lp["rw_a2"]], axis=1)], axis=0).astype(BF16)
    return dict(mu=row(lp["shift_mu"]), w0=row(lp["rw_w0"]), a0=row(lp["rw_a0"]), w2a=w2a,
                k_k=row(lp["rw_k_k"]), k_a=row(lp["rw_k_a"]), r_k=row(lp["rw_r_k"]),
                ln_g=row(lp["rw_ln_g"]), ln_b=row(lp["rw_ln_b"]))


def _rwkv_prompt(proj, prev_row, s0_bd, rwp, tb):
    t = proj.shape[0]
    gsum, lincl, ones = _rw_consts()
    n = RW_GROUP * RW_CHUNK
    const = lambda shape: pl.BlockSpec(shape, lambda i: (0,) * len(shape))
    vec = const((1, RW_WIDTH))
    return pl.pallas_call(
        functools.partial(_rwkv_prompt_kernel, tb=tb),
        grid=(t // tb,),
        in_specs=[pl.BlockSpec((tb, RW_SHIFT_WIDTH), lambda i: (i, 0)),
                  const((1, RW_SHIFT_WIDTH)), const((RW_NGROUP, n, n)), const((1, RW_SHIFT_WIDTH)),
                  vec, vec, const((2 * RW_LORA, 2 * RW_WIDTH)), vec, vec, vec, vec, vec,
                  const((RW_WIDTH, RW_WIDTH)), const((RW_CHUNK, RW_CHUNK)), const((RW_CHUNK, 128))],
        out_specs=[pl.BlockSpec((tb, RW_WIDTH), lambda i: (i, 0)), const((RW_NGROUP, n, n))],
        out_shape=[jax.ShapeDtypeStruct((t, RW_WIDTH), F32), jax.ShapeDtypeStruct((RW_NGROUP, n, n), F32)],
        scratch_shapes=[pltpu.VMEM((RW_NGROUP, n, n), F32), pltpu.VMEM((1, RW_SHIFT_WIDTH), F32)]
        + [pltpu.VMEM((tb, RW_WIDTH), F32)] * 7,
        compiler_params=_cparams(1),
        name="rwkv_prompt",
    )(proj, prev_row, s0_bd, rwp["mu"], rwp["w0"], rwp["a0"], rwp["w2a"], rwp["k_k"], rwp["k_a"], rwp["r_k"],
      rwp["ln_g"], rwp["ln_b"], gsum, lincl, ones)


def _state_to_bd(s):
    st = jnp.swapaxes(s, -1, -2).reshape(RW_NGROUP, RW_GROUP, RW_HEAD_DIM, RW_HEAD_DIM)
    eye = jnp.eye(RW_GROUP, dtype=s.dtype)
    bd = st[:, :, :, None, :] * eye[None, :, None, :, None]
    n = RW_GROUP * RW_HEAD_DIM
    return bd.reshape(RW_NGROUP, n, n)


def _bd_to_state(bd):
    n = RW_GROUP * RW_HEAD_DIM
    b5 = bd.reshape(RW_NGROUP, RW_GROUP, RW_HEAD_DIM, RW_GROUP, RW_HEAD_DIM)
    idx = jnp.arange(RW_GROUP)
    blocks = b5[:, idx, :, idx, :]
    blocks = jnp.swapaxes(blocks, 0, 1).reshape(RW_HEADS, RW_HEAD_DIM, RW_HEAD_DIM)
    return jnp.swapaxes(blocks, -1, -2)


def _rwkv_sample_prep_kernel(p_ref, prev_ref, mu_ref, w0_ref, a0_ref, w2a_ref, kk_ref, ka_ref, rk_ref, gsum_ref,
                             r_ref, w_ref, k_ref, v_ref, kko_ref, b_ref, bonus_ref):
    r, lw, k, v, kk, b, bonus = _rw_prep(p_ref[...], prev_ref[...], mu_ref[...], w0_ref[...], a0_ref[...],
                                         w2a_ref[...], kk_ref[...], ka_ref[...], rk_ref[...], gsum_ref[...])
    r_ref[...] = r
    w_ref[...] = jnp.exp(lw)
    k_ref[...] = k
    v_ref[...] = v
    kko_ref[...] = kk
    b_ref[...] = b
    bonus_ref[...] = bonus


def _rwkv_sample_state_kernel(s_ref, r_ref, w_ref, k_ref, kk_ref, b_ref, v_ref, bonus_ref, lng_ref, lnb_ref,
                              y_ref, so_ref):
    s = s_ref[...]
    sa = jnp.sum(s * kk_ref[...], axis=-1, keepdims=True)
    s = s * w_ref[...] - sa * b_ref[...] + v_ref[...] * k_ref[...]
    so_ref[...] = s
    y = jnp.sum(s * r_ref[...], axis=-1, keepdims=True)
    mean = jnp.mean(y, axis=-2, keepdims=True)
    d = y - mean
    var = jnp.mean(d * d, axis=-2, keepdims=True)
    y_ref[...] = d * lax.rsqrt(var + GN_EPS) * lng_ref[...] + lnb_ref[...] + bonus_ref[...]


def _rwkv_sample(proj, prev_rows, state, rwp):
    b = proj.shape[0]
    gsum, _, _ = _rw_consts()
    full = lambda shape: pl.BlockSpec(shape, lambda i: (0,) * len(shape))
    vec = full((1, RW_WIDTH))
    outs = pl.pallas_call(
        _rwkv_sample_prep_kernel,
        grid=(1,),
        in_specs=[pl.BlockSpec((b, RW_SHIFT_WIDTH), lambda i: (0, 0)),
                  full((b, RW_SHIFT_WIDTH)), full((1, RW_SHIFT_WIDTH)), vec, vec,
                  full((2 * RW_LORA, 2 * RW_WIDTH)), vec, vec, vec, full((RW_WIDTH, RW_WIDTH))],
        out_specs=[full((b, RW_WIDTH))] * 7,
        out_shape=[jax.ShapeDtypeStruct((b, RW_WIDTH), F32)] * 7,
        compiler_params=_cparams(1),
        name="rwkv_sample_prep",
    )(proj, prev_rows, rwp["mu"], rwp["w0"], rwp["a0"], rwp["w2a"], rwp["k_k"], rwp["k_a"], rwp["r_k"], gsum)
    r, w, k, v, kk, bb, bonus = outs
    as_row = lambda a: a.reshape(b, RW_HEADS, 1, RW_HEAD_DIM)
    as_col = lambda a: a.reshape(-1, RW_HEADS, RW_HEAD_DIM, 1)
    nb = SAMPLE_ROWS_PER_STEP
    row_spec = pl.BlockSpec((nb, RW_HEADS, 1, RW_HEAD_DIM), lambda i: (i, 0, 0, 0))
    col_spec = pl.BlockSpec((nb, RW_HEADS, RW_HEAD_DIM, 1), lambda i: (i, 0, 0, 0))
    par_spec = pl.BlockSpec((RW_HEADS, RW_HEAD_DIM, 1), lambda i: (0, 0, 0))
    s_spec = pl.BlockSpec((nb, RW_HEADS, RW_HEAD_DIM, RW_HEAD_DIM), lambda i: (i, 0, 0, 0))
    y, s_new = pl.pallas_call(
        _rwkv_sample_state_kernel,
        grid=(b // nb,),
        in_specs=[s_spec, row_spec, row_spec, row_spec, row_spec, row_spec, col_spec, col_spec, par_spec, par_spec],
        out_specs=[col_spec, s_spec],
        out_shape=[jax.ShapeDtypeStruct((b, RW_HEADS, RW_HEAD_DIM, 1), F32),
                   jax.ShapeDtypeStruct(state.shape, F32)],
        compiler_params=_cparams(1),
        name="rwkv_sample_state",
    )(state, as_row(r), as_row(w), as_row(k), as_row(kk), as_row(bb), as_col(v), as_col(bonus),
      rwp["ln_g"].reshape(RW_HEADS, RW_HEAD_DIM, 1), rwp["ln_b"].reshape(RW_HEADS, RW_HEAD_DIM, 1))
    return y.reshape(b, RW_WIDTH), s_new


def _mem_prompt_kernel(q_ref, mk_ref, mv_ref, o_ref):
    scale = MEM_HEAD_DIM ** -0.5
    for h in range(MEM_HEADS):
        sl = slice(h * MEM_HEAD_DIM, (h + 1) * MEM_HEAD_DIM)
        s = _dot_nt(q_ref[:, sl].astype(BF16), mk_ref[:, sl].astype(BF16)) * scale
        p = jnp.exp(s - jnp.max(s, axis=1, keepdims=True))
        l = jnp.sum(p, axis=1, keepdims=True)
        o_ref[:, sl] = _dot(p.astype(BF16), mv_ref[:, sl].astype(BF16)) / l


def _mem_prompt(proj, mk, mv, tm):
    t = proj.shape[0]
    m = mk.shape[0]
    return pl.pallas_call(
        _mem_prompt_kernel,
        grid=(t // tm,),
        in_specs=[pl.BlockSpec((tm, MEM_WIDTH), lambda i: (i, COL_MQ // MEM_WIDTH)),
                  pl.BlockSpec((m, MEM_WIDTH), lambda i: (0, 0)),
                  pl.BlockSpec((m, MEM_WIDTH), lambda i: (0, 0))],
        out_specs=pl.BlockSpec((tm, MEM_WIDTH), lambda i: (i, 0)),
        out_shape=jax.ShapeDtypeStruct((t, MEM_WIDTH), F32),
        compiler_params=_cparams(1),
        name="mem_prompt",
    )(proj, mk, mv)


def _mem_sample_kernel(q_ref, mk_ref, mv_ref, o_ref):
    scale = MEM_HEAD_DIM ** -0.5
    rowh = lax.broadcasted_iota(jnp.int32, (8, MEM_WIDTH), 0)
    laneh = lax.broadcasted_iota(jnp.int32, (8, MEM_WIDTH), 1) // MEM_HEAD_DIM
    own = rowh == laneh
    for j in range(q_ref.shape[0]):
        q = q_ref[j]
        qblk = jnp.where(own, q, 0.0).astype(BF16)
        s = _dot_nt(qblk, mk_ref[j].astype(BF16)) * scale
        p = jnp.exp(s - jnp.max(s, axis=1, keepdims=True))
        l = jnp.sum(p, axis=1, keepdims=True)
        o = _dot(p.astype(BF16), mv_ref[j].astype(BF16)) / l
        o_ref[j] = jnp.sum(jnp.where(own, o, 0.0), axis=0, keepdims=True)


def _mem_sample(q, mk, mv):
    b, m, _ = mk.shape
    nb = SAMPLE_ROWS_PER_STEP
    q_spec = pl.BlockSpec((nb, 1, MEM_WIDTH), lambda i: (i, 0, 0))
    kv_spec = pl.BlockSpec((nb, m, MEM_WIDTH), lambda i: (i, 0, 0))
    out = pl.pallas_call(
        _mem_sample_kernel,
        grid=(b // nb,),
        in_specs=[q_spec, kv_spec, kv_spec],
        out_specs=q_spec,
        out_shape=jax.ShapeDtypeStruct((b, 1, MEM_WIDTH), F32),
        compiler_params=_cparams(1),
        name="mem_sample",
    )(q.reshape(b, 1, MEM_WIDTH), mk, mv)
    return out.reshape(b, MEM_WIDTH)


def _merge_kernel(x_ref, gl_ref, osb_ref, sz_ref, orw_ref, rz_ref, omem_ref, mz_ref,
                  wsb_ref, wrw_ref, wmem_ref, wo_ref, fg_ref, o_ref, *, final):
    a_sb = (osb_ref[...] * _silu(sz_ref[...])).astype(BF16)
    a_rw = (orw_ref[...] * _silu(rz_ref[...])).astype(BF16)
    a_mem = (omem_ref[...] * _silu(mz_ref[...])).astype(BF16)
    merged = (_sigmoid(gl_ref[:, 0:D_MODEL]) * _dot(a_sb, wsb_ref[...])
              + _sigmoid(gl_ref[:, D_MODEL:2 * D_MODEL]) * _dot(a_rw, wrw_ref[...])
              + _sigmoid(gl_ref[:, 2 * D_MODEL:3 * D_MODEL]) * _dot(a_mem, wmem_ref[...]))
    y = x_ref[...] + _dot(merged.astype(BF16), wo_ref[...])
    if final:
        ms = jnp.mean(y * y, axis=-1, keepdims=True)
        y = y * lax.rsqrt(ms + RMS_EPS) * fg_ref[...]
    o_ref[...] = y


def _merge(x, proj, o_sb, o_rw, o_mem, w_sb, w_rw, w_mem, w_o, final_g, final, tm, name):
    t = x.shape[0]
    col = lambda c: pl.BlockSpec((tm, SB_WIDTH), lambda i: (i, c // SB_WIDTH))
    act = pl.BlockSpec((tm, SB_WIDTH), lambda i: (i, 0))
    wspec = pl.BlockSpec((SB_WIDTH, D_MODEL), lambda i: (0, 0))
    return pl.pallas_call(
        functools.partial(_merge_kernel, final=final),
        grid=(t // tm,),
        in_specs=[pl.BlockSpec((tm, D_MODEL), lambda i: (i, 0)),
                  pl.BlockSpec((tm, 3 * D_MODEL), lambda i: (i, 0)),
                  act, col(COL_SZ), act, col(COL_RZ), act, col(COL_MZ),
                  wspec, wspec, wspec, pl.BlockSpec((D_MODEL, D_MODEL), lambda i: (0, 0)),
                  pl.BlockSpec((1, D_MODEL), lambda i: (0, 0))],
        out_specs=pl.BlockSpec((tm, D_MODEL), lambda i: (i, 0)),
        out_shape=jax.ShapeDtypeStruct((t, D_MODEL), F32),
        compiler_params=_cparams(1),
        name=name,
    )(x, proj, o_sb, proj, o_rw, proj, o_mem, proj, w_sb, w_rw, w_mem, w_o, final_g.reshape(1, D_MODEL))


def _split_w_in(w_in):
    sq, sk, sv, sz, rp, rz, mq, mz, gl = jnp.split(
        w_in, np.cumsum([SB_WIDTH] * 4 + [RW_SHIFT_WIDTH, RW_WIDTH, MEM_WIDTH, MEM_WIDTH]).tolist(), axis=1)
    w_act = jnp.concatenate([gl, sq, sz, rz, mq, mz], axis=1).astype(BF16)
    return w_act, rp.astype(BF16), sk.T.astype(BF16), sv.T.astype(BF16)


def _pages_from_t(xt, page):
    t = xt.shape[1]
    x = xt.reshape(SB_HEADS, SB_HEAD_DIM, t // page, page)
    return jnp.transpose(x, (2, 3, 0, 1))[None]


def kernel(x_prompt, x_sample, cache_sb_k, cache_sb_v, cache_mem_k, cache_mem_v, state_wkv, state_shift, page_table, mem_prompt, norm_g, w_in, sb_bias, shift_mu, rw_w0, rw_w2, rw_a0, rw_a2, rw_k_k, rw_k_a, rw_r_k, rw_ln_g, rw_ln_b, mem_norm_g, w_mem_kv, w_bo_sb, w_bo_rw, w_bo_mem, w_o, final_norm_g):
    depth = w_in.shape[0]
    bp, tp, _ = x_prompt.shape
    bs, ts, _ = x_sample.shape
    assert bp == 1 and ts == 1
    page = cache_sb_k.shape[2]
    n_mem = mem_prompt.shape[1]
    sb_scale = SB_HEAD_DIM ** -0.5

    hp = x_prompt.reshape(tp, D_MODEL)
    hs = x_sample.reshape(bs, D_MODEL)
    mem = mem_prompt.reshape(n_mem, D_MODEL)
    k_pool_t = jnp.transpose(cache_sb_k, (0, 1, 3, 4, 2))
    v_pool_t = jnp.transpose(cache_sb_v, (0, 1, 3, 4, 2))
    outs = {k: [] for k in ("sbk_p", "sbv_p", "mk_p", "mv_p", "wkv_p", "shift_p", "sbk_s", "sbv_s", "wkv_s", "shift_s")}
    for l in range(depth):
        last = l == depth - 1
        lp = dict(shift_mu=shift_mu[l], rw_w0=rw_w0[l], rw_w2=rw_w2[l], rw_a0=rw_a0[l], rw_a2=rw_a2[l],
                  rw_k_k=rw_k_k[l], rw_k_a=rw_k_a[l], rw_r_k=rw_r_k[l], rw_ln_g=rw_ln_g[l], rw_ln_b=rw_ln_b[l])
        rwp = _rw_params(lp)
        w_act, w_rp, w_kt, w_vt = _split_w_in(w_in[l])
        w_sb, w_rw, w_mem, w_out = (w.astype(BF16) for w in (w_bo_sb[l], w_bo_rw[l], w_bo_mem[l], w_o[l]))

        act, rp, kt, vt = _in_proj(hp, norm_g[l], w_act, w_rp, w_kt, w_vt, 512, "proj_prompt")
        kv = _norm_proj(mem, mem_norm_g[l], w_mem_kv[l].astype(BF16), n_mem, MEM_WIDTH, "mem_kv")
        mk, mv = kv[:, :MEM_WIDTH], kv[:, MEM_WIDTH:]
        o_sb = _sb_prompt(act, kt.reshape(SB_HEADS, SB_HEAD_DIM, tp), vt.reshape(SB_HEADS, SB_HEAD_DIM, tp),
                          sb_bias[l], 512, 2048)
        s0 = _state_to_bd(jnp.zeros((RW_HEADS, RW_HEAD_DIM, RW_HEAD_DIM), F32))
        o_rw, s_bd = _rwkv_prompt(rp, jnp.zeros((1, RW_SHIFT_WIDTH), F32), s0, rwp, 512)
        o_mem = _mem_prompt(act, mk, mv, 512)
        hp = _merge(hp, act, o_sb, o_rw, o_mem, w_sb, w_rw, w_mem, w_out, final_norm_g, last, 256, "merge_prompt")
        outs["sbk_p"].append(_pages_from_t(kt, page))
        outs["sbv_p"].append(_pages_from_t(vt, page))
        outs["mk_p"].append(mk.reshape(bp, n_mem, MEM_HEADS, MEM_HEAD_DIM))
        outs["mv_p"].append(mv.reshape(bp, n_mem, MEM_HEADS, MEM_HEAD_DIM))
        outs["wkv_p"].append(_bd_to_state(s_bd).reshape(bp, RW_HEADS, RW_HEAD_DIM, RW_HEAD_DIM))
        outs["shift_p"].append(rp[tp - 1:tp])

        act_s, rp_s, kt_s, vt_s = _in_proj(hs, norm_g[l], w_act, w_rp, w_kt, w_vt, bs, "proj_sample")
        o_sb_s = _sb_sample(act_s[:, COL_SQ:COL_SQ + SB_WIDTH] * sb_scale, sb_bias[l], k_pool_t, v_pool_t, l,
                            page_table, 32)
        o_rw_s, s_new = _rwkv_sample(rp_s, state_shift[l], state_wkv[l], rwp)
        o_mem_s = _mem_sample(act_s[:, COL_MQ:COL_MQ + MEM_WIDTH],
                              cache_mem_k[l].reshape(bs, n_mem, MEM_WIDTH),
                              cache_mem_v[l].reshape(bs, n_mem, MEM_WIDTH))
        hs = _merge(hs, act_s, o_sb_s, o_rw_s, o_mem_s, w_sb, w_rw, w_mem, w_out, final_norm_g, last, bs,
                    "merge_sample")
        outs["sbk_s"].append(kt_s.T.reshape(bs, ts, SB_HEADS, SB_HEAD_DIM))
        outs["sbv_s"].append(vt_s.T.reshape(bs, ts, SB_HEADS, SB_HEAD_DIM))
        outs["wkv_s"].append(s_new)
        outs["shift_s"].append(rp_s)

    st = {k: jnp.stack(v) for k, v in outs.items()}
    return (hp.reshape(bp, tp, D_MODEL), hs.reshape(bs, ts, D_MODEL),
            st["sbk_p"], st["sbv_p"], st["mk_p"], st["mv_p"], st["wkv_p"], st["shift_p"],
            st["sbk_s"], st["sbv_s"], st["wkv_s"], st["shift_s"])
```
